```python
import jax, jax.numpy as jnp
from jax import lax
import numpy as np

D_MODEL = 1024
BATCH = 32
SEQ = 2048
DEPTH = 1
DEC_BATCH = 16
DEC_SEQ = 2048
PAST_LEN = 128

N_MEM = 256
GRID_W = 64
HEAD_DIM = 64
A_GROUPS = ((128, 1), (512, 4), (2048, 16))
A_N_GROUPS = 3
A_HEADS = 8
A_WIDTH = A_HEADS * HEAD_DIM
B_HEADS = 8
B_WIDTH = B_HEADS * HEAD_DIM
NA_ROWS = 8
NA_COLS = 16
NA_Q_COLS = 16
NA_K_COLS = 32
M_HEADS = 4
M_HEAD_DIM = 128
M_WIDTH = M_HEADS * M_HEAD_DIM
N_BRANCH = 3
SPLITS = (A_N_GROUPS * A_WIDTH, A_N_GROUPS * A_WIDTH, A_N_GROUPS * A_WIDTH, A_WIDTH,
          B_WIDTH, B_WIDTH, B_WIDTH, B_WIDTH, M_WIDTH, M_WIDTH, N_BRANCH * D_MODEL)
D_IN = sum(SPLITS)
RMS_EPS = 1e-6
NEG_INF = -1e30

kernel_name = 'hybrid_dilated_neighbourhood_memory_encoder'


def _rmsnorm(x, g):
    x32 = x.astype(jnp.float32)
    y = x32 * lax.rsqrt(jnp.mean(x32 * x32, axis=-1, keepdims=True) + RMS_EPS)
    return (y * g.astype(jnp.float32)).astype(x.dtype)


def _dilated_group(q, k, v, window, dilation, slopes):
    b, s, h, dh = q.shape
    reach = window // (2 * dilation)
    L = s // dilation
    nb = -(-L // reach)
    Lp = nb * reach

    def sub(t):
        return t.reshape(b, L, dilation, h, dh).transpose(0, 2, 1, 3, 4)

    qb = jnp.pad(sub(q), ((0, 0), (0, 0), (0, Lp - L), (0, 0), (0, 0))).reshape(b, dilation, nb, reach, h, dh)

    def kv_blocks(t):
        tp = jnp.pad(sub(t), ((0, 0), (0, 0), (reach, Lp - L + reach), (0, 0), (0, 0)))
        tp = tp.reshape(b, dilation, nb + 2, reach, h, dh)
        return jnp.concatenate([tp[:, :, 0:nb], tp[:, :, 1:nb + 1], tp[:, :, 2:nb + 2]], axis=3)

    kb, vb = kv_blocks(k), kv_blocks(v)
    scores = jnp.einsum('bdnqhe,bdnkhe->bhdnqk', qb, kb) * (dh ** -0.5)
    qi = np.arange(reach)[:, None]
    ki = np.arange(3 * reach)[None, :]
    off = ki - reach - qi
    kpos = np.arange(nb)[:, None, None] * reach + ki[None] - reach
    valid = (np.abs(off) <= reach)[None] & (kpos >= 0) & (kpos < L)
    dist = (np.abs(off) * dilation).astype(np.float32)
    alibi = -slopes[:, None, None] * dist
    sc = jnp.where(valid[None, None, None], scores + alibi[None, :, None, None], NEG_INF)
    m = jnp.max(sc, axis=-1, keepdims=True)
    p = jnp.exp(sc - m)
    l = jnp.sum(p, axis=-1)
    o = jnp.einsum('bhdnqk,bdnkhe->bhdnqe', p, vb)

    def unsub(t):
        rest = t.shape[5:]
        t = t.reshape(b, h, dilation, Lp, *rest)[:, :, :, :L]
        t = t.transpose((0, 3, 2, 1) + tuple(range(4, t.ndim)))
        return t.reshape(b, s, h, *rest)

    return unsub(m[..., 0]), unsub(l), unsub(o)


def _dilated_mixer(q, k, v):
    n_heads = A_N_GROUPS * A_HEADS
    slopes = jnp.exp2(-8.0 * jnp.arange(1, n_heads + 1, dtype=jnp.float32) / n_heads).reshape(A_N_GROUPS, A_HEADS)
    ms, ls, os_ = [], [], []
    for g, (window, dilation) in enumerate(A_GROUPS):
        m, l, o = _dilated_group(q[:, :, g], k[:, :, g], v[:, :, g], window, dilation, slopes[g])
        ms.append(m); ls.append(l); os_.append(o)
    mm = jnp.stack(ms)
    w = jnp.exp(mm - jnp.max(mm, axis=0, keepdims=True))
    num = jnp.sum(w[..., None] * jnp.stack(os_), axis=0)
    den = jnp.sum(w * jnp.stack(ls), axis=0)
    return num / den[..., None]


def _neighbourhood_attention(q, k, v, rpb):
    b, s, h, dh = q.shape
    rows = s // GRID_W
    wr = min(NA_ROWS, rows)
    n_cb = GRID_W // NA_Q_COLS
    qc = np.arange(GRID_W).reshape(n_cb, NA_Q_COLS)
    sc = np.clip(qc - NA_COLS // 2, 0, GRID_W - NA_COLS)
    kstart = np.minimum(sc[:, 0], GRID_W - NA_K_COLS)
    kc = kstart[:, None] + np.arange(NA_K_COLS)[None]
    dc = kc[:, None, :] - qc[:, :, None]
    col_valid = (kc[:, None, :] >= sc[:, :, None]) & (kc[:, None, :] < sc[:, :, None] + NA_COLS)
    dc_idx = np.clip(dc + NA_COLS - 1, 0, 2 * NA_COLS - 2)
    col_bias = rpb[:, :, dc_idx]
    kg = k.reshape(b, rows, GRID_W, h, dh)
    vg = v.reshape(b, rows, GRID_W, h, dh)
    qrows = q.reshape(b, rows, n_cb, NA_Q_COLS, h, dh).transpose(1, 0, 2, 3, 4, 5)
    scale = dh ** -0.5

    def row_fn(args):
        r, q_r = args
        r0 = jnp.clip(r - wr // 2, 0, rows - wr)
        k_r = lax.dynamic_slice_in_dim(kg, r0, wr, axis=1)[:, :, kc]
        v_r = lax.dynamic_slice_in_dim(vg, r0, wr, axis=1)[:, :, kc]
        scores = jnp.einsum('bcqhe,bwcmhe->bhcqwm', q_r, k_r) * scale
        dr_idx = r0 + jnp.arange(wr) - r + NA_ROWS - 1
        bias = jnp.take(col_bias, dr_idx, axis=1).transpose(0, 2, 3, 1, 4)
        sc_ = jnp.where(col_valid[None, None, :, :, None, :], scores + bias[None], NEG_INF)
        p = jax.nn.softmax(sc_.reshape(b, h, n_cb, NA_Q_COLS, wr * NA_K_COLS), axis=-1)
        p = p.reshape(b, h, n_cb, NA_Q_COLS, wr, NA_K_COLS)
        return jnp.einsum('bhcqwm,bwcmhe->bcqhe', p, v_r)

    out = lax.map(row_fn, (jnp.arange(rows), qrows))
    return out.transpose(1, 0, 2, 3, 4, 5).reshape(b, s, h, dh)


def _memory_attention(q, mk, mv):
    scores = jnp.einsum('bshe,bnhe->bhsn', q, mk) * (q.shape[-1] ** -0.5)
    p = jax.nn.softmax(scores, axis=-1)
    return jnp.einsum('bhsn,bnhe->bshe', p, mv)


def _layer(x, mem, g_norm, g_mem, w_in, w_mem_kv, rpb, w_pa, w_pb, w_pm, w_out):
    b, s, _ = x.shape
    f32 = jnp.float32
    h = _rmsnorm(x, g_norm)
    proj = h @ w_in
    split_points = np.cumsum(SPLITS)[:-1].tolist()
    aq, ak, av, ag, bq, bk, bv, bg, mq, mg, merge = jnp.split(proj, split_points, axis=-1)

    def heads(t, *hd):
        return t.reshape(b, t.shape[1], *hd).astype(f32)

    a_out = _dilated_mixer(heads(aq, A_N_GROUPS, A_HEADS, HEAD_DIM),
                           heads(ak, A_N_GROUPS, A_HEADS, HEAD_DIM),
                           heads(av, A_N_GROUPS, A_HEADS, HEAD_DIM))
    b_out = _neighbourhood_attention(heads(bq, B_HEADS, HEAD_DIM), heads(bk, B_HEADS, HEAD_DIM),
                                     heads(bv, B_HEADS, HEAD_DIM), rpb.astype(f32))
    mem_h = _rmsnorm(mem, g_mem)
    mk, mv = jnp.split(mem_h @ w_mem_kv, 2, axis=-1)
    m_out = _memory_attention(heads(mq, M_HEADS, M_HEAD_DIM), heads(mk, M_HEADS, M_HEAD_DIM),
                              heads(mv, M_HEADS, M_HEAD_DIM))

    branch_a = (a_out.reshape(b, s, A_WIDTH).astype(x.dtype) * jax.nn.silu(ag)) @ w_pa
    branch_b = (b_out.reshape(b, s, B_WIDTH).astype(x.dtype) * jax.nn.silu(bg)) @ w_pb
    branch_m = (m_out.reshape(b, s, M_WIDTH).astype(x.dtype) * jax.nn.silu(mg)) @ w_pm
    gates = jax.nn.sigmoid(merge).reshape(b, s, N_BRANCH, D_MODEL)
    merged = gates[:, :, 0] * branch_a + gates[:, :, 1] * branch_b + gates[:, :, 2] * branch_m
    return x + merged @ w_out


def setup_inputs(seed: int = 0) -> dict:
    key = jax.random.key(seed)
    ks = jax.random.split(key, 16)
    nrm = jax.random.normal
    f32 = jnp.float32
    return {
        'x_prompt': nrm(ks[0], (BATCH, SEQ, D_MODEL), f32),
        'x_sample': nrm(ks[1], (DEC_BATCH, DEC_SEQ, D_MODEL), f32),
        'mem_prompt': nrm(ks[2], (BATCH, N_MEM, D_MODEL), f32),
        'mem_sample': nrm(ks[3], (DEC_BATCH, N_MEM, D_MODEL), f32),
        'norm_gain': 1.0 + 0.1 * nrm(ks[4], (DEPTH, D_MODEL), f32),
        'mem_norm_gain': 1.0 + 0.1 * nrm(ks[5], (DEPTH, D_MODEL), f32),
        'w_in': nrm(ks[6], (DEPTH, D_MODEL, D_IN), f32) * D_MODEL ** -0.5,
        'w_mem_kv': nrm(ks[7], (DEPTH, D_MODEL, 2 * M_WIDTH), f32) * D_MODEL ** -0.5,
        'rpb': 0.5 * nrm(ks[8], (DEPTH, B_HEADS, 2 * NA_ROWS - 1, 2 * NA_COLS - 1), f32),
        'w_proj_a': nrm(ks[9], (DEPTH, A_WIDTH, D_MODEL), f32) * A_WIDTH ** -0.5,
        'w_proj_b': nrm(ks[10], (DEPTH, B_WIDTH, D_MODEL), f32) * B_WIDTH ** -0.5,
        'w_proj_m': nrm(ks[11], (DEPTH, M_WIDTH, D_MODEL), f32) * M_WIDTH ** -0.5,
        'w_out': nrm(ks[12], (DEPTH, D_MODEL, D_MODEL), f32) * D_MODEL ** -0.5,
        'final_norm_gain': 1.0 + 0.1 * nrm(ks[13], (D_MODEL,), f32),
    }


def reference(x_prompt, x_sample, mem_prompt, mem_sample, norm_gain, mem_norm_gain, w_in, w_mem_kv,
              rpb, w_proj_a, w_proj_b, w_proj_m, w_out, final_norm_gain):
    def trunk(x, mem):
        for i in range(DEPTH):
            x = _layer(x, mem, norm_gain[i], mem_norm_gain[i], w_in[i], w_mem_kv[i], rpb[i],
                       w_proj_a[i], w_proj_b[i], w_proj_m[i], w_out[i])
        return _rmsnorm(x, final_norm_gain)

    y_prompt = trunk(x_prompt, mem_prompt)
    y_sample = trunk(x_sample, mem_sample)
    return (y_prompt, y_sample)
```

```python
import functools

import numpy as np
import jax
import jax.numpy as jnp
from jax import lax
from jax.experimental import pallas as pl
from jax.experimental.pallas import tpu as pltpu

F32 = jnp.float32
BF16 = jnp.bfloat16

D_MODEL = 1024
SEQ = 2048
N_MEM = 256
GRID_W = 64
ROWS = SEQ // GRID_W
HEAD_DIM = 64
A_GROUPS = ((128, 1), (512, 4), (2048, 16))
A_HEADS = 8
NA_ROWS = 8
NA_COLS = 16
M_HEADS = 4
M_HEAD_DIM = 128
RMS_EPS = 1e-6
NEG_INF = -1e30

LANES = 128
N_PAIRS = A_HEADS // 2
QT = 128
A_REACH = 64
B_KROWS = 10
B_KEYS = B_KROWS * GRID_W
B_VARIANTS = (0, 1, 2, 14, 15)
M_QT = 256

NAT_COLS = 8192
COL_A_QKV = 0
COL_B_QKV = 1536
COL_AG = 3072
COL_BG = 3584
COL_MQ = 4096
COL_MG = 4608
COL_MERGE = 5120
DIL_COLS = 1536

VMEM_LIMIT = 52 * 1024 * 1024


def _silu(x):
    return x * jax.nn.sigmoid(x)


def _dot_nt(a, b):
    return lax.dot_general(a, b, (((1,), (1,)), ((), ())), preferred_element_type=F32)


def _norm_proj_kernel(x_ref, g_ref, w_ref, o_ref, h_ref, *, dil, seq):
    sub = seq // dil
    rows = min(sub, 256)

    @pl.when(pl.program_id(1) == 0)
    def _():
        g = g_ref[...]
        for r in range(dil):
            for c in range(sub // rows):
                xs = x_ref[0, c * rows:(c + 1) * rows, r * D_MODEL:(r + 1) * D_MODEL]
                ms = jnp.mean(xs * xs, axis=-1, keepdims=True)
                h = (xs * lax.rsqrt(ms + RMS_EPS)) * g
                h_ref[r * sub + c * rows:r * sub + (c + 1) * rows, :] = h.astype(BF16)

    mt = min(seq, 512)
    for c in range(seq // mt):
        acc = jnp.dot(h_ref[c * mt:(c + 1) * mt, :], w_ref[...], preferred_element_type=F32)
        o_ref[0, c * mt:(c + 1) * mt, :] = acc.astype(BF16)


def _norm_proj(x, gain, w, *, dil, tn):
    b, seq, _ = x.shape
    n = w.shape[1]
    sub = seq // dil
    xv = x.reshape(b, sub, dil * D_MODEL)
    return pl.pallas_call(
        functools.partial(_norm_proj_kernel, dil=dil, seq=seq),
        grid=(b, n // tn),
        in_specs=[
            pl.BlockSpec((1, sub, dil * D_MODEL), lambda i, j: (i, 0, 0)),
            pl.BlockSpec((1, D_MODEL), lambda i, j: (0, 0)),
            pl.BlockSpec((D_MODEL, tn), lambda i, j: (0, j)),
        ],
        out_specs=pl.BlockSpec((1, seq, tn), lambda i, j: (i, 0, j)),
        out_shape=jax.ShapeDtypeStruct((b, seq, n), BF16),
        scratch_shapes=[pltpu.VMEM((seq, D_MODEL), BF16)],
        compiler_params=pltpu.CompilerParams(
            dimension_semantics=("arbitrary", "arbitrary"), vmem_limit_bytes=VMEM_LIMIT),
        name=f"norm_proj_d{dil}",
    )(xv, gain.reshape(1, D_MODEL), w)


def _pair_attention(q, k, vext, biases, head0):
    outs = []
    for hh in range(2):
        keep = head0 if hh == 0 else jnp.logical_not(head0)
        qm = jnp.where(keep, q, jnp.zeros_like(q))
        s = _dot_nt(qm, k) + biases[hh]
        m = jnp.max(s, axis=-1, keepdims=True)
        p = jnp.exp(s - m).astype(BF16)
        oe = jnp.dot(p, vext, preferred_element_type=F32)
        outs.append((oe, m))
    (oe0, m0), (oe1, m1) = outs
    o = jnp.where(head0, oe0[:, :LANES], oe1[:, :LANES])
    l = jnp.where(head0, oe0[:, LANES:], oe1[:, LANES:])
    m = jnp.where(head0, m0, m1)
    return o, l, m


def _dilated_kernel(coef_ref, q0_ref, q1_ref, q2_ref, ag_ref, out_ref,
                    vext_ref, bias_ref, o_ref, l_ref, m_ref):
    hp = pl.program_id(1)
    head0 = lax.broadcasted_iota(jnp.int32, (QT, LANES), 1) < HEAD_DIM
    vext_ref[:, LANES:] = jnp.ones((SEQ, LANES), BF16)
    scale = HEAD_DIM ** -0.5

    for g, ((_, dil), qkv_ref) in enumerate(zip(A_GROUPS, (q0_ref, q1_ref, q2_ref))):
        sub = SEQ // dil
        nt = sub // QT
        tk = min(2 * QT, sub)
        offsets = (0, -A_REACH, -2 * A_REACH) if nt > 1 else (0,)

        vext_ref[:, :LANES] = qkv_ref[0, :, 2 * LANES:3 * LANES]
        row = lax.broadcasted_iota(jnp.int32, (QT, tk), 0)
        col = lax.broadcasted_iota(jnp.int32, (QT, tk), 1)
        for v, off in enumerate(offsets):
            rel = jnp.abs(col + off - row)
            relf = rel.astype(F32)
            for hh in range(2):
                c = coef_ref[g * A_HEADS + 2 * hp + hh]
                bias_ref[2 * v + hh, :, :tk] = jnp.where(rel > A_REACH, NEG_INF, -c * relf)

        def tile(u, carry, g=g, dil=dil, sub=sub, nt=nt, tk=tk, qkv_ref=qkv_ref):
            if nt == 1:
                r, t, vidx = u, 0, 0
                ks = 0
            else:
                r, t = u // nt, u % nt
                ks = jnp.clip(t * QT - A_REACH, 0, sub - tk)
                vidx = jnp.where(t == 0, 0, jnp.where(t == nt - 1, 2, 1))
            qrow = pl.multiple_of(r * sub + t * QT, QT)
            krow = pl.multiple_of(r * sub + ks, A_REACH)
            q = qkv_ref[0, pl.ds(qrow, QT), 0:LANES] * jnp.asarray(scale, BF16)
            k = qkv_ref[0, pl.ds(krow, tk), LANES:2 * LANES]
            vext = vext_ref[pl.ds(krow, tk), :]
            biases = [bias_ref[2 * vidx + hh, :, :tk] for hh in range(2)]
            o, l, m = _pair_attention(q, k, vext, biases, head0)
            if dil == 1:
                dst = pl.ds(pl.multiple_of(t * QT, QT), QT)
            else:
                dst = pl.ds(r + dil * QT * t, QT, stride=dil)
            o_ref[g, dst, :] = o
            l_ref[g, dst, :] = l
            m_ref[g, dst, :] = m
            return carry

        lax.fori_loop(0, SEQ // QT, tile, 0)

    def combine(u, carry):
        sl = pl.ds(pl.multiple_of(u * QT, QT), QT)
        ms = [m_ref[g, sl, :] for g in range(3)]
        mm = jnp.maximum(jnp.maximum(ms[0], ms[1]), ms[2])
        ws = [jnp.exp(m - mm) for m in ms]
        num = ws[0] * o_ref[0, sl, :] + ws[1] * o_ref[1, sl, :] + ws[2] * o_ref[2, sl, :]
        den = ws[0] * l_ref[0, sl, :] + ws[1] * l_ref[1, sl, :] + ws[2] * l_ref[2, sl, :]
        gate = ag_ref[0, sl, :].astype(F32)
        out_ref[0, sl, :] = ((num / den) * _silu(gate)).astype(BF16)
        return carry

    lax.fori_loop(0, SEQ // QT, combine, 0)


def _dilated_mixer(coef, proj_nat, proj_d4, proj_d16):
    b = proj_nat.shape[0]
    qkv_spec = pl.BlockSpec((1, SEQ, 3 * LANES), lambda i, j, c: (i, 0, j))
    grid_spec = pltpu.PrefetchScalarGridSpec(
        num_scalar_prefetch=1,
        grid=(b, N_PAIRS),
        in_specs=[
            pl.BlockSpec((1, SEQ, 3 * LANES), lambda i, j, c: (i, 0, COL_A_QKV // (3 * LANES) + j)),
            qkv_spec,
            qkv_spec,
            pl.BlockSpec((1, SEQ, LANES), lambda i, j, c: (i, 0, COL_AG // LANES + j)),
        ],
        out_specs=pl.BlockSpec((1, SEQ, LANES), lambda i, j, c: (i, 0, j)),
        scratch_shapes=[
            pltpu.VMEM((SEQ, 2 * LANES), BF16),
            pltpu.VMEM((6, QT, 2 * QT), F32),
            pltpu.VMEM((3, SEQ, LANES), F32),
            pltpu.VMEM((3, SEQ, LANES), F32),
            pltpu.VMEM((3, SEQ, LANES), F32),
        ],
    )
    return pl.pallas_call(
        _dilated_kernel,
        grid_spec=grid_spec,
        out_shape=jax.ShapeDtypeStruct((b, SEQ, N_PAIRS * LANES), BF16),
        compiler_params=pltpu.CompilerParams(
            dimension_semantics=("arbitrary", "arbitrary"), vmem_limit_bytes=VMEM_LIMIT),
        name="dilated_mixer",
    )(coef, proj_nat, proj_d4, proj_d16, proj_nat)


def _neighbour_kernel(qkv_ref, bg_ref, tab_ref, out_ref, vext_ref):
    head0 = lax.broadcasted_iota(jnp.int32, (QT, LANES), 1) < HEAD_DIM
    vext_ref[:, LANES:] = jnp.ones((SEQ, LANES), BF16)
    vext_ref[:, :LANES] = qkv_ref[0, :, 2 * LANES:3 * LANES]
    scale = HEAD_DIM ** -0.5
    n_tiles = SEQ // QT

    def tile(t, carry):
        ksr = jnp.clip(2 * t - NA_ROWS // 2, 0, ROWS - B_KROWS)
        krow = pl.multiple_of(ksr * GRID_W, GRID_W)
        vidx = jnp.where(t < 2, t, jnp.where(t > n_tiles - 3, t - (n_tiles - 5), 2))
        sl = pl.ds(pl.multiple_of(t * QT, QT), QT)
        q = qkv_ref[0, sl, 0:LANES] * jnp.asarray(scale, BF16)
        k = qkv_ref[0, pl.ds(krow, B_KEYS), LANES:2 * LANES]
        vext = vext_ref[pl.ds(krow, B_KEYS), :]
        biases = [tab_ref[0, vidx, hh] for hh in range(2)]
        o, l, _ = _pair_attention(q, k, vext, biases, head0)
        gate = bg_ref[0, sl, :].astype(F32)
        out_ref[0, sl, :] = ((o / l) * _silu(gate)).astype(BF16)
        return carry

    lax.fori_loop(0, n_tiles, tile, 0)


def _neighbour_table(rpb):
    wr = min(NA_ROWS, ROWS)
    qi = np.arange(QT)
    ki = np.arange(B_KEYS)
    rr, c = qi // GRID_W, qi % GRID_W
    w, kc = ki // GRID_W, ki % GRID_W
    dr_l, dc_l, ok_l = [], [], []
    for t in B_VARIANTS:
        r = 2 * t + rr
        ks = np.clip(2 * t - NA_ROWS // 2, 0, ROWS - B_KROWS)
        keyrow = ks + w
        r0 = np.clip(r - wr // 2, 0, ROWS - wr)
        row_ok = (keyrow[None, :] >= r0[:, None]) & (keyrow[None, :] < r0[:, None] + wr)
        sc = np.clip(c - NA_COLS // 2, 0, GRID_W - NA_COLS)
        col_ok = (kc[None, :] >= sc[:, None]) & (kc[None, :] < sc[:, None] + NA_COLS)
        dr_l.append(np.clip(keyrow[None, :] - r[:, None] + NA_ROWS - 1, 0, 2 * NA_ROWS - 2))
        dc_l.append(np.clip(kc[None, :] - c[:, None] + NA_COLS - 1, 0, 2 * NA_COLS - 2))
        ok_l.append(row_ok & col_ok)
    dr, dc, ok = np.stack(dr_l), np.stack(dc_l), np.stack(ok_l)
    tab = jnp.where(ok[None], rpb[:, dr, dc], NEG_INF)
    tab = tab.reshape(N_PAIRS, 2, len(B_VARIANTS), QT, B_KEYS)
    return tab.transpose(0, 2, 1, 3, 4)


def _neighbour_mixer(proj_nat, table):
    b = proj_nat.shape[0]
    return pl.pallas_call(
        _neighbour_kernel,
        grid=(N_PAIRS, b),
        in_specs=[
            pl.BlockSpec((1, SEQ, 3 * LANES), lambda j, i: (i, 0, COL_B_QKV // (3 * LANES) + j)),
            pl.BlockSpec((1, SEQ, LANES), lambda j, i: (i, 0, COL_BG // LANES + j)),
            pl.BlockSpec((1, len(B_VARIANTS), 2, QT, B_KEYS), lambda j, i: (j, 0, 0, 0, 0)),
        ],
        out_specs=pl.BlockSpec((1, SEQ, LANES), lambda j, i: (i, 0, j)),
        out_shape=jax.ShapeDtypeStruct((b, SEQ, N_PAIRS * LANES), BF16),
        scratch_shapes=[pltpu.VMEM((SEQ, 2 * LANES), BF16)],
        compiler_params=pltpu.CompilerParams(
            dimension_semantics=("arbitrary", "arbitrary"), vmem_limit_bytes=VMEM_LIMIT),
        name="neighbour_mixer",
    )(proj_nat, proj_nat, table)


def _memory_kernel(q_ref, mg_ref, mk_ref, mv_ref, out_ref, vext_ref):
    vext_ref[:, LANES:] = jnp.ones((N_MEM, LANES), BF16)
    vext_ref[:, :LANES] = mv_ref[0]
    scale = M_HEAD_DIM ** -0.5

    def tile(t, carry):
        sl = pl.ds(pl.multiple_of(t * M_QT, M_QT), M_QT)
        s = _dot_nt(q_ref[0, sl, :], mk_ref[0]) * scale
        m = jnp.max(s, axis=-1, keepdims=True)
        p = jnp.exp(s - m).astype(BF16)
        oe = jnp.dot(p, vext_ref[...], preferred_element_type=F32)
        gate = mg_ref[0, sl, :].astype(F32)
        out_ref[0, sl, :] = ((oe[:, :LANES] / oe[:, LANES:]) * _silu(gate)).astype(BF16)
        return carry

    lax.fori_loop(0, SEQ // M_QT, tile, 0)


def _memory_mixer(proj_nat, mem_kv):
    b = proj_nat.shape[0]
    return pl.pallas_call(
        _memory_kernel,
        grid=(b, M_HEADS),
        in_specs=[
            pl.BlockSpec((1, SEQ, LANES), lambda i, j: (i, 0, COL_MQ // LANES + j)),
            pl.BlockSpec((1, SEQ, LANES), lambda i, j: (i, 0, COL_MG // LANES + j)),
            pl.BlockSpec((1, N_MEM, LANES), lambda i, j: (i, 0, j)),
            pl.BlockSpec((1, N_MEM, LANES), lambda i, j: (i, 0, M_HEADS + j)),
        ],
        out_specs=pl.BlockSpec((1, SEQ, LANES), lambda i, j: (i, 0, j)),
        out_shape=jax.ShapeDtypeStruct((b, SEQ, M_HEADS * LANES), BF16),
        scratch_shapes=[pltpu.VMEM((N_MEM, 2 * LANES), BF16)],
        compiler_params=pltpu.CompilerParams(
            dimension_semantics=("arbitrary", "arbitrary"), vmem_limit_bytes=VMEM_LIMIT),
        name="memory_mixer",
    )(proj_nat, proj_nat, mem_kv, mem_kv)


def _merge_out_kernel(ga_ref, gb_ref, gm_ref, s0_ref, s1_ref, s2_ref, x_ref,
                      wa_ref, wb_ref, wm_ref, wo_ref, g_ref, out_ref):
    merged = None
    for br_ref, w_ref, s_ref in ((ga_ref, wa_ref, s0_ref), (gb_ref, wb_ref, s1_ref),
                                 (gm_ref, wm_ref, s2_ref)):
        branch = jnp.dot(br_ref[0], w_ref[...], preferred_element_type=F32)
        term = jax.nn.sigmoid(s_ref[0].astype(F32)) * branch
        merged = term if merged is None else merged + term
    y = x_ref[0] + jnp.dot(merged.astype(BF16), wo_ref[...], preferred_element_type=F32)
    ms = jnp.mean(y * y, axis=-1, keepdims=True)
    out_ref[0] = (y * lax.rsqrt(ms + RMS_EPS)) * g_ref[...]


def _merge_out(ga, gb, gm, proj_nat, x, wa, wb, wm, wo, gain, *, tm=512):
    b = x.shape[0]
    width = ga.shape[-1]
    br_spec = pl.BlockSpec((1, tm, width), lambda i, j: (i, j, 0))
    w_spec = pl.BlockSpec((width, D_MODEL), lambda i, j: (0, 0))
    gate_specs = [
        pl.BlockSpec((1, tm, D_MODEL), functools.partial(lambda i, j, n: (i, j, COL_MERGE // D_MODEL + n), n=n))
        for n in range(3)
    ]
    return pl.pallas_call(
        _merge_out_kernel,
        grid=(b, SEQ // tm),
        in_specs=[br_spec, br_spec, br_spec, *gate_specs,
                  pl.BlockSpec((1, tm, D_MODEL), lambda i, j: (i, j, 0)),
                  w_spec, w_spec, w_spec,
                  pl.BlockSpec((D_MODEL, D_MODEL), lambda i, j: (0, 0)),
                  pl.BlockSpec((1, D_MODEL), lambda i, j: (0, 0))],
        out_specs=pl.BlockSpec((1, tm, D_MODEL), lambda i, j: (i, j, 0)),
        out_shape=jax.ShapeDtypeStruct((b, SEQ, D_MODEL), F32),
        compiler_params=pltpu.CompilerParams(
            dimension_semantics=("arbitrary", "arbitrary"), vmem_limit_bytes=VMEM_LIMIT),
        name="merge_out",
    )(ga, gb, gm, proj_nat, proj_nat, proj_nat, x, wa, wb, wm, wo, gain.reshape(1, D_MODEL))


def _prepare_weights(w_in):
    a_w = 3 * A_HEADS * HEAD_DIM
    b_w = A_HEADS * HEAD_DIM
    aq, ak, av = (w_in[:, i * a_w:(i + 1) * a_w].reshape(D_MODEL, 3, N_PAIRS, LANES) for i in range(3))
    a_qkv = jnp.stack([aq, ak, av], axis=3)
    off = 3 * a_w
    ag = w_in[:, off:off + b_w]
    off += b_w
    bq, bk, bv = (w_in[:, off + i * b_w:off + (i + 1) * b_w].reshape(D_MODEL, N_PAIRS, LANES) for i in range(3))
    b_qkv = jnp.stack([bq, bk, bv], axis=2).reshape(D_MODEL, 3 * b_w)
    off += 3 * b_w
    bg = w_in[:, off:off + b_w]
    rest = w_in[:, off + b_w:]
    w_nat = jnp.concatenate([a_qkv[:, 0].reshape(D_MODEL, a_w), b_qkv, ag, bg, rest], axis=1)
    w_d4 = a_qkv[:, 1].reshape(D_MODEL, a_w)
    w_d16 = a_qkv[:, 2].reshape(D_MODEL, a_w)
    return w_nat.astype(BF16), w_d4.astype(BF16), w_d16.astype(BF16)


def _alibi_coefficients():
    n_heads = len(A_GROUPS) * A_HEADS
    slopes = jnp.exp2(-8.0 * jnp.arange(1, n_heads + 1, dtype=F32) / n_heads)
    dils = np.repeat(np.array([d for _, d in A_GROUPS], np.float32), A_HEADS)
    return slopes * dils


def _trunk(x, mem, weights):
    (g_norm, g_mem, w_nat, w_d4, w_d16, w_mem, table, coef, wa, wb, wm, wo, g_final) = weights
    proj_nat = _norm_proj(x, g_norm, w_nat, dil=1, tn=1024)
    proj_d4 = _norm_proj(x, g_norm, w_d4, dil=4, tn=DIL_COLS)
    proj_d16 = _norm_proj(x, g_norm, w_d16, dil=16, tn=DIL_COLS)
    mem_kv = _norm_proj(mem, g_mem, w_mem, dil=1, tn=1024)
    ga = _dilated_mixer(coef, proj_nat, proj_d4, proj_d16)
    gb = _neighbour_mixer(proj_nat, table)
    gm = _memory_mixer(proj_nat, mem_kv)
    return _merge_out(ga, gb, gm, proj_nat, x, wa, wb, wm, wo, g_final)


def kernel(x_prompt, x_sample, mem_prompt, mem_sample, norm_gain, mem_norm_gain, w_in, w_mem_kv,
           rpb, w_proj_a, w_proj_b, w_proj_m, w_out, final_norm_gain):
    assert norm_gain.shape[0] == 1, "single-layer trunk"
    w_nat, w_d4, w_d16 = _prepare_weights(w_in[0])
    weights = (norm_gain[0], mem_norm_gain[0], w_nat, w_d4, w_d16, w_mem_kv[0].astype(BF16),
               _neighbour_table(rpb[0]), _alibi_coefficients(),
               w_proj_a[0].astype(BF16), w_proj_b[0].astype(BF16), w_proj_m[0].astype(BF16),
               w_out[0].astype(BF16), final_norm_gain)
    return (_trunk(x_prompt, mem_prompt, weights), _trunk(x_sample, mem_sample, weights))
```

```python
import functools

import numpy as np
import jax
import jax.numpy as jnp
from jax import lax
from jax.experimental import pallas as pl
from jax.experimental.pallas import tpu as pltpu

F32 = jnp.float32
BF16 = jnp.bfloat16

D_MODEL = 1024
SEQ = 2048
N_MEM = 256
GRID_W = 64
ROWS = SEQ // GRID_W
HEAD_DIM = 64
A_GROUPS = ((128, 1), (512, 4), (2048, 16))
A_HEADS = 8
NA_ROWS = 8
NA_COLS = 16
M_HEADS = 4
M_HEAD_DIM = 128
RMS_EPS = 1e-6
NEG_INF = -1e30

LANES = 128
N_PAIRS = A_HEADS // 2
QT = 128
STRIDE_STEP = 4
A_REACH = 64
B_KROWS = 10
B_KEYS = B_KROWS * GRID_W
B_VARIANTS = (0, 1, 2, 14, 15)
M_QT = 256
A_UNROLL = 8
B_UNROLL = 4
M_UNROLL = 4

NAT_COLS = 8192
COL_A_QKV = 0
COL_B_QKV = 1536
COL_AG = 3072
COL_BG = 3584
COL_MQ = 4096
COL_MG = 4608
COL_MERGE = 5120
DIL_COLS = 1536

VMEM_LIMIT = 52 * 1024 * 1024


def _silu(x):
    return x * jax.nn.sigmoid(x)


def _dot_nt(a, b):
    return lax.dot_general(a, b, (((1,), (1,)), ((), ())), preferred_element_type=F32)


def _norm_proj_kernel(x_ref, g_ref, w_ref, o_ref, h_ref, *, dil, seq):
    sub = seq // dil
    rows = min(sub, 256)

    @pl.when(pl.program_id(1) == 0)
    def _():
        g = g_ref[...]
        for r in range(dil):
            for c in range(sub // rows):
                xs = x_ref[0, c * rows:(c + 1) * rows, r * D_MODEL:(r + 1) * D_MODEL]
                ms = jnp.mean(xs * xs, axis=-1, keepdims=True)
                h = (xs * lax.rsqrt(ms + RMS_EPS)) * g
                h_ref[r * sub + c * rows:r * sub + (c + 1) * rows, :] = h.astype(BF16)

    mt = min(seq, 512)
    for c in range(seq // mt):
        acc = jnp.dot(h_ref[c * mt:(c + 1) * mt, :], w_ref[...], preferred_element_type=F32)
        o_ref[0, c * mt:(c + 1) * mt, :] = acc.astype(BF16)


def _norm_proj(x, gain, w, *, dil, tn):
    b, seq, _ = x.shape
    n = w.shape[1]
    sub = seq // dil
    xv = x.reshape(b, sub, dil * D_MODEL)
    return pl.pallas_call(
        functools.partial(_norm_proj_kernel, dil=dil, seq=seq),
        grid=(b, n // tn),
        in_specs=[
            pl.BlockSpec((1, sub, dil * D_MODEL), lambda i, j: (i, 0, 0)),
            pl.BlockSpec((1, D_MODEL), lambda i, j: (0, 0)),
            pl.BlockSpec((D_MODEL, tn), lambda i, j: (0, j)),
        ],
        out_specs=pl.BlockSpec((1, seq, tn), lambda i, j: (i, 0, j)),
        out_shape=jax.ShapeDtypeStruct((b, seq, n), BF16),
        scratch_shapes=[pltpu.VMEM((seq, D_MODEL), BF16)],
        compiler_params=pltpu.CompilerParams(
            dimension_semantics=("arbitrary", "arbitrary"), vmem_limit_bytes=VMEM_LIMIT),
        name=f"norm_proj_d{dil}",
    )(xv, gain.reshape(1, D_MODEL), w)


def _pair_attention(q, k, vext, biases, head0):
    outs = []
    for hh in range(2):
        keep = head0 if hh == 0 else jnp.logical_not(head0)
        qm = jnp.where(keep, q, jnp.zeros_like(q))
        s = _dot_nt(qm, k) + biases[hh]
        m = jnp.max(s, axis=-1, keepdims=True)
        p = jnp.exp(s - m).astype(BF16)
        oe = jnp.dot(p, vext, preferred_element_type=F32)
        outs.append((oe, m))
    (oe0, m0), (oe1, m1) = outs
    o = jnp.where(head0, oe0[:, :LANES], oe1[:, :LANES])
    l = jnp.where(head0, oe0[:, LANES:], oe1[:, LANES:])
    m = jnp.where(head0, m0, m1)
    return o, l, m


def _dilated_kernel(coef_ref, q0_ref, q1_ref, q2_ref, ag_ref, out_ref,
                    vext_ref, bias_ref, o_ref, l_ref, m_ref, stage_ref):
    hp = pl.program_id(1)
    head0 = lax.broadcasted_iota(jnp.int32, (QT, LANES), 1) < HEAD_DIM
    vext_ref[:, LANES:] = jnp.ones((SEQ, LANES), BF16)
    scale = HEAD_DIM ** -0.5

    for g, ((_, dil), qkv_ref) in enumerate(zip(A_GROUPS, (q0_ref, q1_ref, q2_ref))):
        sub = SEQ // dil
        nt = sub // QT
        tk = 2 * QT
        if nt > 1:
            variants = ((0, 0, tk), (-A_REACH, 0, tk), (-2 * A_REACH, 0, tk))
        else:
            variants = ((0, 0, QT), (-QT, QT, tk))

        vext_ref[:, :LANES] = qkv_ref[0, :, 2 * LANES:3 * LANES]
        row = lax.broadcasted_iota(jnp.int32, (QT, tk), 0)
        col = lax.broadcasted_iota(jnp.int32, (QT, tk), 1)
        for v, (off, lo, hi) in enumerate(variants):
            rel = jnp.abs(col + off - row)
            relf = rel.astype(F32)
            masked = (rel > A_REACH) | (col < lo) | (col >= hi)
            for hh in range(2):
                c = coef_ref[g * A_HEADS + 2 * hp + hh]
                bias_ref[2 * v + hh] = jnp.where(masked, NEG_INF, -c * relf)

        def tile(u, carry, g=g, dil=dil, sub=sub, nt=nt, tk=tk, qkv_ref=qkv_ref):
            if nt == 1:
                r, t, vidx = u, 0, u % 2
                ks = (u % 2) * -QT
            else:
                r, t = u // nt, u % nt
                ks = jnp.clip(t * QT - A_REACH, 0, sub - tk)
                vidx = jnp.where(t == 0, 0, jnp.where(t == nt - 1, 2, 1))
            qrow = pl.multiple_of(r * sub + t * QT, QT)
            krow = pl.multiple_of(r * sub + ks, A_REACH)
            q = qkv_ref[0, pl.ds(qrow, QT), 0:LANES] * jnp.asarray(scale, BF16)
            k = qkv_ref[0, pl.ds(krow, tk), LANES:2 * LANES]
            vext = vext_ref[pl.ds(krow, tk), :]
            biases = [bias_ref[2 * vidx + hh, :, :tk] for hh in range(2)]
            o, l, m = _pair_attention(q, k, vext, biases, head0)
            if dil == 1:
                dst = pl.ds(pl.multiple_of(t * QT, QT), QT)
            elif dil == STRIDE_STEP:
                dst = pl.ds(r + dil * QT * t, QT, stride=dil)
            else:
                dst = pl.ds((r % STRIDE_STEP) * (SEQ // STRIDE_STEP) + r // STRIDE_STEP, QT, stride=STRIDE_STEP)
            if dil == STRIDE_STEP ** 2:
                stage_ref[0, dst, :] = o
                stage_ref[1, dst, :] = l
                stage_ref[2, dst, :] = m
            else:
                o_ref[g, dst, :] = o
                l_ref[g, dst, :] = l
                m_ref[g, dst, :] = m
            return carry

        lax.fori_loop(0, SEQ // QT, tile, 0, unroll=A_UNROLL)

        if dil == STRIDE_STEP ** 2:
            quarter = SEQ // STRIDE_STEP
            for k, dst_ref in enumerate((o_ref, l_ref, m_ref)):
                for ra in range(STRIDE_STEP):
                    dst_ref[g, pl.ds(ra, quarter, stride=STRIDE_STEP), :] = (
                        stage_ref[k, ra * quarter:(ra + 1) * quarter, :])

    def combine(u, carry):
        sl = pl.ds(pl.multiple_of(u * QT, QT), QT)
        ms = [m_ref[g, sl, :] for g in range(3)]
        mm = jnp.maximum(jnp.maximum(ms[0], ms[1]), ms[2])
        ws = [jnp.exp(m - mm) for m in ms]
        num = ws[0] * o_ref[0, sl, :] + ws[1] * o_ref[1, sl, :] + ws[2] * o_ref[2, sl, :]
        den = ws[0] * l_ref[0, sl, :] + ws[1] * l_ref[1, sl, :] + ws[2] * l_ref[2, sl, :]
        gate = ag_ref[0, sl, :].astype(F32)
        out_ref[0, sl, :] = ((num / den) * _silu(gate)).astype(BF16)
        return carry

    lax.fori_loop(0, SEQ // QT, combine, 0)


def _dilated_mixer(coef, proj_nat, proj_d4, proj_d16):
    b = proj_nat.shape[0]
    qkv_spec = pl.BlockSpec((1, SEQ, 3 * LANES), lambda i, j, c: (i, 0, j))
    grid_spec = pltpu.PrefetchScalarGridSpec(
        num_scalar_prefetch=1,
        grid=(b, N_PAIRS),
        in_specs=[
            pl.BlockSpec((1, SEQ, 3 * LANES), lambda i, j, c: (i, 0, COL_A_QKV // (3 * LANES) + j)),
            qkv_spec,
            qkv_spec,
            pl.BlockSpec((1, SEQ, LANES), lambda i, j, c: (i, 0, COL_AG // LANES + j)),
        ],
        out_specs=pl.BlockSpec((1, SEQ, LANES), lambda i, j, c: (i, 0, j)),
        scratch_shapes=[
            pltpu.VMEM((SEQ, 2 * LANES), BF16),
            pltpu.VMEM((6, QT, 2 * QT), F32),
            pltpu.VMEM((3, SEQ, LANES), F32),
            pltpu.VMEM((3, SEQ, LANES), F32),
            pltpu.VMEM((3, SEQ, LANES), F32),
            pltpu.VMEM((3, SEQ, LANES), F32),
        ],
    )
    return pl.pallas_call(
        _dilated_kernel,
        grid_spec=grid_spec,
        out_shape=jax.ShapeDtypeStruct((b, SEQ, N_PAIRS * LANES), BF16),
        compiler_params=pltpu.CompilerParams(
            dimension_semantics=("arbitrary", "arbitrary"), vmem_limit_bytes=VMEM_LIMIT),
        name="dilated_mixer",
    )(coef, proj_nat, proj_d4, proj_d16, proj_nat)


def _neighbour_kernel(qkv_ref, bg_ref, tab_ref, out_ref, vext_ref):
    head0 = lax.broadcasted_iota(jnp.int32, (QT, LANES), 1) < HEAD_DIM
    vext_ref[:, LANES:] = jnp.ones((SEQ, LANES), BF16)
    vext_ref[:, :LANES] = qkv_ref[0, :, 2 * LANES:3 * LANES]
    scale = HEAD_DIM ** -0.5
    n_tiles = SEQ // QT

    def tile(t, carry):
        ksr = jnp.clip(2 * t - NA_ROWS // 2, 0, ROWS - B_KROWS)
        krow = pl.multiple_of(ksr * GRID_W, GRID_W)
        vidx = jnp.where(t < 2, t, jnp.where(t > n_tiles - 3, t - (n_tiles - 5), 2))
        sl = pl.ds(pl.multiple_of(t * QT, QT), QT)
        q = qkv_ref[0, sl, 0:LANES] * jnp.asarray(scale, BF16)
        k = qkv_ref[0, pl.ds(krow, B_KEYS), LANES:2 * LANES]
        vext = vext_ref[pl.ds(krow, B_KEYS), :]
        biases = [tab_ref[0, vidx, hh] for hh in range(2)]
        o, l, _ = _pair_attention(q, k, vext, biases, head0)
        gate = bg_ref[0, sl, :].astype(F32)
        out_ref[0, sl, :] = ((o / l) * _silu(gate)).astype(BF16)
        return carry

    lax.fori_loop(0, n_tiles, tile, 0, unroll=B_UNROLL)


def _neighbour_table(rpb):
    wr = min(NA_ROWS, ROWS)
    pad = GRID_W - NA_COLS
    rpb_pad = jnp.pad(rpb, ((0, 0), (0, 0), (pad, pad)))
    toeplitz = jnp.stack([rpb_pad[:, :, GRID_W - 1 - c:2 * GRID_W - 1 - c] for c in range(GRID_W)], axis=2)
    qi = np.arange(QT)
    ki = np.arange(B_KEYS)
    rr, c = qi // GRID_W, qi % GRID_W
    w, kc = ki // GRID_W, ki % GRID_W
    sc = np.clip(c - NA_COLS // 2, 0, GRID_W - NA_COLS)
    col_ok = (kc[None, :] >= sc[:, None]) & (kc[None, :] < sc[:, None] + NA_COLS)
    variants, ok_l = [], []
    for t in B_VARIANTS:
        ks = int(np.clip(2 * t - NA_ROWS // 2, 0, ROWS - B_KROWS))
        halves = []
        for q_row in (2 * t, 2 * t + 1):
            blocks = [toeplitz[:, int(np.clip(ks + kr - q_row + NA_ROWS - 1, 0, 2 * NA_ROWS - 2))]
                      for kr in range(B_KROWS)]
            halves.append(jnp.stack(blocks, axis=2).reshape(A_HEADS, GRID_W, B_KEYS))
        variants.append(jnp.concatenate(halves, axis=1))
        r = 2 * t + rr
        r0 = np.clip(r - wr // 2, 0, ROWS - wr)
        keyrow = ks + w
        row_ok = (keyrow[None, :] >= r0[:, None]) & (keyrow[None, :] < r0[:, None] + wr)
        ok_l.append(row_ok & col_ok)
    tab = jnp.where(np.stack(ok_l)[None], jnp.stack(variants, axis=1), NEG_INF)
    tab = tab.reshape(N_PAIRS, 2, len(B_VARIANTS), QT, B_KEYS)
    return tab.transpose(0, 2, 1, 3, 4)


def _neighbour_mixer(proj_nat, table):
    b = proj_nat.shape[0]
    return pl.pallas_call(
        _neighbour_kernel,
        grid=(N_PAIRS, b),
        in_specs=[
            pl.BlockSpec((1, SEQ, 3 * LANES), lambda j, i: (i, 0, COL_B_QKV // (3 * LANES) + j)),
            pl.BlockSpec((1, SEQ, LANES), lambda j, i: (i, 0, COL_BG // LANES + j)),
            pl.BlockSpec((1, len(B_VARIANTS), 2, QT, B_KEYS), lambda j, i: (j, 0, 0, 0, 0)),
        ],
        out_specs=pl.BlockSpec((1, SEQ, LANES), lambda j, i: (i, 0, j)),
        out_shape=jax.ShapeDtypeStruct((b, SEQ, N_PAIRS * LANES), BF16),
        scratch_shapes=[pltpu.VMEM((SEQ, 2 * LANES), BF16)],
        compiler_params=pltpu.CompilerParams(
            dimension_semantics=("arbitrary", "arbitrary"), vmem_limit_bytes=VMEM_LIMIT),
        name="neighbour_mixer",
    )(proj_nat, proj_nat, table)


def _memory_kernel(q_ref, mg_ref, mk_ref, mv_ref, out_ref, vext_ref):
    vext_ref[:, LANES:] = jnp.ones((N_MEM, LANES), BF16)
    vext_ref[:, :LANES] = mv_ref[0]
    scale = M_HEAD_DIM ** -0.5

    def tile(t, carry):
        sl = pl.ds(pl.multiple_of(t * M_QT, M_QT), M_QT)
        s = _dot_nt(q_ref[0, sl, :], mk_ref[0]) * scale
        m = jnp.max(s, axis=-1, keepdims=True)
        p = jnp.exp(s - m).astype(BF16)
        oe = jnp.dot(p, vext_ref[...], preferred_element_type=F32)
        gate = mg_ref[0, sl, :].astype(F32)
        out_ref[0, sl, :] = ((oe[:, :LANES] / oe[:, LANES:]) * _silu(gate)).astype(BF16)
        return carry

    lax.fori_loop(0, SEQ // M_QT, tile, 0, unroll=M_UNROLL)


def _memory_mixer(proj_nat, mem_kv):
    b = proj_nat.shape[0]
    return pl.pallas_call(
        _memory_kernel,
        grid=(b, M_HEADS),
        in_specs=[
            pl.BlockSpec((1, SEQ, LANES), lambda i, j: (i, 0, COL_MQ // LANES + j)),
            pl.BlockSpec((1, SEQ, LANES), lambda i, j: (i, 0, COL_MG // LANES + j)),
            pl.BlockSpec((1, N_MEM, LANES), lambda i, j: (i, 0, j)),
            pl.BlockSpec((1, N_MEM, LANES), lambda i, j: (i, 0, M_HEADS + j)),
        ],
        out_specs=pl.BlockSpec((1, SEQ, LANES), lambda i, j: (i, 0, j)),
        out_shape=jax.ShapeDtypeStruct((b, SEQ, M_HEADS * LANES), BF16),
        scratch_shapes=[pltpu.VMEM((N_MEM, 2 * LANES), BF16)],
        compiler_params=pltpu.CompilerParams(
            dimension_semantics=("arbitrary", "arbitrary"), vmem_limit_bytes=VMEM_LIMIT),
        name="memory_mixer",
    )(proj_nat, proj_nat, mem_kv, mem_kv)


def _merge_out_kernel(ga_ref, gb_ref, gm_ref, s0_ref, s1_ref, s2_ref, x_ref,
                      wa_ref, wb_ref, wm_ref, wo_ref, g_ref, out_ref):
    merged = None
    for br_ref, w_ref, s_ref in ((ga_ref, wa_ref, s0_ref), (gb_ref, wb_ref, s1_ref),
                                 (gm_ref, wm_ref, s2_ref)):
        branch = jnp.dot(br_ref[0], w_ref[...], preferred_element_type=F32)
        term = jax.nn.sigmoid(s_ref[0].astype(F32)) * branch
        merged = term if merged is None else merged + term
    y = x_ref[0] + jnp.dot(merged.astype(BF16), wo_ref[...], preferred_element_type=F32)
    ms = jnp.mean(y * y, axis=-1, keepdims=True)
    out_ref[0] = (y * lax.rsqrt(ms + RMS_EPS)) * g_ref[...]


def _merge_out(ga, gb, gm, proj_nat, x, wa, wb, wm, wo, gain, *, tm=512):
    b = x.shape[0]
    width = ga.shape[-1]
    br_spec = pl.BlockSpec((1, tm, width), lambda i, j: (i, j, 0))
    w_spec = pl.BlockSpec((width, D_MODEL), lambda i, j: (0, 0))
    gate_specs = [
        pl.BlockSpec((1, tm, D_MODEL), functools.partial(lambda i, j, n: (i, j, COL_MERGE // D_MODEL + n), n=n))
        for n in range(3)
    ]
    return pl.pallas_call(
        _merge_out_kernel,
        grid=(b, SEQ // tm),
        in_specs=[br_spec, br_spec, br_spec, *gate_specs,
                  pl.BlockSpec((1, tm, D_MODEL), lambda i, j: (i, j, 0)),
                  w_spec, w_spec, w_spec,
                  pl.BlockSpec((D_MODEL, D_MODEL), lambda i, j: (0, 0)),
                  pl.BlockSpec((1, D_MODEL), lambda i, j: (0, 0))],
        out_specs=pl.BlockSpec((1, tm, D_MODEL), lambda i, j: (i, j, 0)),
        out_shape=jax.ShapeDtypeStruct((b, SEQ, D_MODEL), F32),
        compiler_params=pltpu.CompilerParams(
            dimension_semantics=("arbitrary", "arbitrary"), vmem_limit_bytes=VMEM_LIMIT),
        name="merge_out",
    )(ga, gb, gm, proj_nat, proj_nat, proj_nat, x, wa, wb, wm, wo, gain.reshape(1, D_MODEL))


def _prepare_weights(w_in):
    a_w = 3 * A_HEADS * HEAD_DIM
    b_w = A_HEADS * HEAD_DIM
    aq, ak, av = (w_in[:, i * a_w:(i + 1) * a_w].reshape(D_MODEL, 3, N_PAIRS, LANES) for i in range(3))
    a_qkv = jnp.stack([aq, ak, av], axis=3)
    off = 3 * a_w
    ag = w_in[:, off:off + b_w]
    off += b_w
    bq, bk, bv = (w_in[:, off + i * b_w:off + (i + 1) * b_w].reshape(D_MODEL, N_PAIRS, LANES) for i in range(3))
    b_qkv = jnp.stack([bq, bk, bv], axis=2).reshape(D_MODEL, 3 * b_w)
    off += 3 * b_w
    bg = w_in[:, off:off + b_w]
    rest = w_in[:, off + b_w:]
    w_nat = jnp.concatenate([a_qkv[:, 0].reshape(D_MODEL, a_w), b_qkv, ag, bg, rest], axis=1)
    w_d4 = a_qkv[:, 1].reshape(D_MODEL, a_w)
    w_d16 = a_qkv[:, 2].reshape(D_MODEL, a_w)
    return w_nat.astype(BF16), w_d4.astype(BF16), w_d16.astype(BF16)


def _alibi_coefficients():
    n_heads = len(A_GROUPS) * A_HEADS
    slopes = jnp.exp2(-8.0 * jnp.arange(1, n_heads + 1, dtype=F32) / n_heads)
    dils = np.repeat(np.array([d for _, d in A_GROUPS], np.float32), A_HEADS)
    return slopes * dils


def _trunk(x, mem, weights):
    (g_norm, g_mem, w_nat, w_d4, w_d16, w_mem, table, coef, wa, wb, wm, wo, g_final) = weights
    proj_nat = _norm_proj(x, g_norm, w_nat, dil=1, tn=1024)
    proj_d4 = _norm_proj(x, g_norm, w_d4, dil=4, tn=DIL_COLS)
    proj_d16 = _norm_proj(x, g_norm, w_d16, dil=16, tn=DIL_COLS)
    mem_kv = _norm_proj(mem, g_mem, w_mem, dil=1, tn=1024)
    ga = _dilated_mixer(coef, proj_nat, proj_d4, proj_d16)
    gb = _neighbour_mixer(proj_nat, table)
    gm = _memory_mixer(proj_nat, mem_kv)
    return _merge_out(ga, gb, gm, proj_nat, x, wa, wb, wm, wo, g_final)


def kernel(x_prompt, x_sample, mem_prompt, mem_sample, norm_gain, mem_norm_gain, w_in, w_mem_kv,
           rpb, w_proj_a, w_proj_b, w_proj_m, w_out, final_norm_gain):
    assert norm_gain.shape[0] == 1, "single-layer trunk"
    w_nat, w_d4, w_d16 = _prepare_weights(w_in[0])
    weights = (norm_gain[0], mem_norm_gain[0], w_nat, w_d4, w_d16, w_mem_kv[0].astype(BF16),
               _neighbour_table(rpb[0]), _alibi_coefficients(),
               w_proj_a[0].astype(BF16), w_proj_b[0].astype(BF16), w_proj_m[0].astype(BF16),
               w_out[0].astype(BF16), final_norm_gain)
    return (_trunk(x_prompt, mem_prompt, weights), _trunk(x_sample, mem_sample, weights))
```

```python
import functools

import numpy as np
import jax
import jax.numpy as jnp
from jax import lax
from jax.experimental import pallas as pl
from jax.experimental.pallas import tpu as pltpu

F32 = jnp.float32
BF16 = jnp.bfloat16

D_MODEL = 1024
SEQ = 2048
N_MEM = 256
GRID_W = 64
ROWS = SEQ // GRID_W
HEAD_DIM = 64
A_GROUPS = ((128, 1), (512, 4), (2048, 16))
A_HEADS = 8
NA_ROWS = 8
NA_COLS = 16
M_HEADS = 4
M_HEAD_DIM = 128
RMS_EPS = 1e-6
NEG_INF = -1e30

LANES = 128
N_PAIRS = A_HEADS // 2
QT = 128
STRIDE_STEP = 4
A_REACH = 64
B_KROWS = 10
B_KEYS = B_KROWS * GRID_W
B_VARIANTS = (0, 1, 2, 14, 15)
M_QT = 256
A_UNROLL = 8
B_UNROLL = 4
M_UNROLL = 4

NAT_COLS = 8192
COL_A_QKV = 0
COL_B_QKV = 1536
COL_AG = 3072
COL_BG = 3584
COL_MQ = 4096
COL_MG = 4608
COL_MERGE = 5120
DIL_COLS = 1536

VMEM_LIMIT = 52 * 1024 * 1024
PROJ_VMEM_LIMIT = 56 * 1024 * 1024


def _silu(x):
    return x * jax.nn.sigmoid(x)


def _dot_nt(a, b):
    return lax.dot_general(a, b, (((1,), (1,)), ((), ())), preferred_element_type=F32)


def _norm_proj_kernel(x_ref, g_ref, w_ref, o_ref, h_ref, *, dil, seq):
    sub = seq // dil
    rows = min(sub, 256)

    @pl.when(pl.program_id(1) == 0)
    def _():
        g = g_ref[...]
        for r in range(dil):
            for c in range(sub // rows):
                xs = x_ref[0, c * rows:(c + 1) * rows, r * D_MODEL:(r + 1) * D_MODEL]
                ms = jnp.mean(xs * xs, axis=-1, keepdims=True)
                h = (xs * lax.rsqrt(ms + RMS_EPS)) * g
                h_ref[r * sub + c * rows:r * sub + (c + 1) * rows, :] = h.astype(BF16)

    mt = min(seq, 512)
    for c in range(seq // mt):
        acc = jnp.dot(h_ref[c * mt:(c + 1) * mt, :], w_ref[...], preferred_element_type=F32)
        o_ref[0, c * mt:(c + 1) * mt, :] = acc.astype(BF16)


def _norm_proj(x, gain, w, *, dil, tn):
    b, seq, _ = x.shape
    n = w.shape[1]
    sub = seq // dil
    xv = x.reshape(b, sub, dil * D_MODEL)
    return pl.pallas_call(
        functools.partial(_norm_proj_kernel, dil=dil, seq=seq),
        grid=(b, n // tn),
        in_specs=[
            pl.BlockSpec((1, sub, dil * D_MODEL), lambda i, j: (i, 0, 0)),
            pl.BlockSpec((1, D_MODEL), lambda i, j: (0, 0)),
            pl.BlockSpec((D_MODEL, tn), lambda i, j: (0, j)),
        ],
        out_specs=pl.BlockSpec((1, seq, tn), lambda i, j: (i, 0, j)),
        out_shape=jax.ShapeDtypeStruct((b, seq, n), BF16),
        scratch_shapes=[pltpu.VMEM((seq, D_MODEL), BF16)],
        compiler_params=pltpu.CompilerParams(
            dimension_semantics=("arbitrary", "arbitrary"), vmem_limit_bytes=VMEM_LIMIT),
        name=f"norm_proj_d{dil}",
    )(xv, gain.reshape(1, D_MODEL), w)


NAT_TN = 1024
N_NAT_STEPS = NAT_COLS // NAT_TN
DIL_TN = 768
N_DIL_STEPS = DIL_COLS // DIL_TN
PROJ_MT = 512


def _fused_proj_kernel(x_ref, g_ref, wn_ref, wd_ref, on_ref, od_ref,
                       hn_ref, hp_ref, inv_ref, sa_ref, sb_ref):
    j = pl.program_id(1)
    quarter = SEQ // STRIDE_STEP
    n_slabs = D_MODEL // LANES

    @pl.when(j == 0)
    def _():
        g = g_ref[...]
        rows = 256
        for c in range(SEQ // rows):
            sl = slice(c * rows, (c + 1) * rows)
            xs = x_ref[0, sl, :]
            inv = lax.rsqrt(jnp.mean(xs * xs, axis=-1, keepdims=True) + RMS_EPS)
            inv_ref[sl, :] = jnp.broadcast_to(inv, (rows, LANES))
            hn_ref[sl, :] = ((xs * inv) * g).astype(BF16)

    def load_normalised_slab(k):
        cols = slice(k * LANES, (k + 1) * LANES)
        sa_ref[...] = (x_ref[0, :, cols] * inv_ref[...]) * g_ref[:, cols]

    @pl.when(j == N_NAT_STEPS)
    def _():
        for k in range(n_slabs):
            load_normalised_slab(k)
            for ra in range(STRIDE_STEP):
                hp_ref[ra * quarter:(ra + 1) * quarter, k * LANES:(k + 1) * LANES] = (
                    sa_ref[pl.ds(ra, quarter, stride=STRIDE_STEP), :].astype(BF16))

    @pl.when(j == N_NAT_STEPS + N_DIL_STEPS)
    def _():
        for k in range(n_slabs):
            load_normalised_slab(k)
            for ra in range(STRIDE_STEP):
                sb_ref[ra * quarter:(ra + 1) * quarter, :] = sa_ref[pl.ds(ra, quarter, stride=STRIDE_STEP), :]
            for slot in range(STRIDE_STEP ** 2):
                ra, rb = slot // STRIDE_STEP, slot % STRIDE_STEP
                hp_ref[slot * QT:(slot + 1) * QT, k * LANES:(k + 1) * LANES] = (
                    sb_ref[pl.ds(ra * quarter + rb, QT, stride=STRIDE_STEP), :].astype(BF16))

    @pl.when(j < N_NAT_STEPS)
    def _():
        for c in range(SEQ // PROJ_MT):
            sl = slice(c * PROJ_MT, (c + 1) * PROJ_MT)
            on_ref[0, sl, :] = jnp.dot(hn_ref[sl, :], wn_ref[...], preferred_element_type=F32).astype(BF16)

    @pl.when(j >= N_NAT_STEPS)
    def _():
        for c in range(SEQ // PROJ_MT):
            sl = slice(c * PROJ_MT, (c + 1) * PROJ_MT)
            od_ref[0, 0, sl, :] = jnp.dot(hp_ref[sl, :], wd_ref[0], preferred_element_type=F32).astype(BF16)


def _fused_proj(x, gain, w_nat, w_dil):
    b = x.shape[0]

    def dil_step(j):
        return jnp.clip(j - N_NAT_STEPS, 0, 2 * N_DIL_STEPS - 1)

    return pl.pallas_call(
        _fused_proj_kernel,
        grid=(b, N_NAT_STEPS + 2 * N_DIL_STEPS),
        in_specs=[
            pl.BlockSpec((1, SEQ, D_MODEL), lambda i, j: (i, 0, 0)),
            pl.BlockSpec((1, D_MODEL), lambda i, j: (0, 0)),
            pl.BlockSpec((D_MODEL, NAT_TN), lambda i, j: (0, jnp.minimum(j, N_NAT_STEPS - 1))),
            pl.BlockSpec((1, D_MODEL, DIL_TN), lambda i, j: (dil_step(j), 0, 0)),
        ],
        out_specs=[
            pl.BlockSpec((1, SEQ, NAT_TN), lambda i, j: (i, 0, jnp.minimum(j, N_NAT_STEPS - 1))),
            pl.BlockSpec((1, 1, SEQ, DIL_TN),
                         lambda i, j: (i, dil_step(j) // N_DIL_STEPS, 0, dil_step(j) % N_DIL_STEPS)),
        ],
        out_shape=[jax.ShapeDtypeStruct((b, SEQ, NAT_COLS), BF16),
                   jax.ShapeDtypeStruct((b, 2, SEQ, DIL_COLS), BF16)],
        scratch_shapes=[
            pltpu.VMEM((SEQ, D_MODEL), BF16),
            pltpu.VMEM((SEQ, D_MODEL), BF16),
            pltpu.VMEM((SEQ, LANES), F32),
            pltpu.VMEM((SEQ, LANES), F32),
            pltpu.VMEM((SEQ, LANES), F32),
        ],
        compiler_params=pltpu.CompilerParams(
            dimension_semantics=("arbitrary", "arbitrary"), vmem_limit_bytes=PROJ_VMEM_LIMIT),
        name="fused_proj",
    )(x, gain.reshape(1, D_MODEL), w_nat, w_dil)


def _pair_attention(q, k, vext, biases, head0):
    outs = []
    for hh in range(2):
        keep = head0 if hh == 0 else jnp.logical_not(head0)
        qm = jnp.where(keep, q, jnp.zeros_like(q))
        s = _dot_nt(qm, k) + biases[hh]
        m = jnp.max(s, axis=-1, keepdims=True)
        p = jnp.exp(s - m).astype(BF16)
        oe = jnp.dot(p, vext, preferred_element_type=F32)
        outs.append((oe, m))
    (oe0, m0), (oe1, m1) = outs
    o = jnp.where(head0, oe0[:, :LANES], oe1[:, :LANES])
    l = jnp.where(head0, oe0[:, LANES:], oe1[:, LANES:])
    m = jnp.where(head0, m0, m1)
    return o, l, m


def _dilated_kernel(coef_ref, q0_ref, q1_ref, q2_ref, ag_ref, out_ref,
                    vext_ref, bias_ref, o_ref, l_ref, m_ref, stage_ref):
    hp = pl.program_id(1)
    head0 = lax.broadcasted_iota(jnp.int32, (QT, LANES), 1) < HEAD_DIM
    vext_ref[:, LANES:] = jnp.ones((SEQ, LANES), BF16)
    scale = HEAD_DIM ** -0.5

    for g, ((_, dil), qkv_ref) in enumerate(zip(A_GROUPS, (q0_ref, q1_ref, q2_ref))):
        sub = SEQ // dil
        nt = sub // QT
        tk = 2 * QT
        if nt > 1:
            variants = ((0, 0, tk), (-A_REACH, 0, tk), (-2 * A_REACH, 0, tk))
        else:
            variants = ((0, 0, QT), (-QT, QT, tk))

        vext_ref[:, :LANES] = qkv_ref[:, 2 * LANES:3 * LANES]
        row = lax.broadcasted_iota(jnp.int32, (QT, tk), 0)
        col = lax.broadcasted_iota(jnp.int32, (QT, tk), 1)
        for v, (off, lo, hi) in enumerate(variants):
            rel = jnp.abs(col + off - row)
            relf = rel.astype(F32)
            masked = (rel > A_REACH) | (col < lo) | (col >= hi)
            for hh in range(2):
                c = coef_ref[g * A_HEADS + 2 * hp + hh]
                bias_ref[2 * v + hh] = jnp.where(masked, NEG_INF, -c * relf)

        def tile(u, carry, g=g, dil=dil, sub=sub, nt=nt, tk=tk, qkv_ref=qkv_ref):
            if nt == 1:
                r, t, vidx = u, 0, u % 2
                ks = (u % 2) * -QT
            else:
                r, t = u // nt, u % nt
                ks = jnp.clip(t * QT - A_REACH, 0, sub - tk)
                vidx = jnp.where(t == 0, 0, jnp.where(t == nt - 1, 2, 1))
            qrow = pl.multiple_of(r * sub + t * QT, QT)
            krow = pl.multiple_of(r * sub + ks, A_REACH)
            q = qkv_ref[pl.ds(qrow, QT), 0:LANES] * jnp.asarray(scale, BF16)
            k = qkv_ref[pl.ds(krow, tk), LANES:2 * LANES]
            vext = vext_ref[pl.ds(krow, tk), :]
            biases = [bias_ref[2 * vidx + hh, :, :tk] for hh in range(2)]
            o, l, m = _pair_attention(q, k, vext, biases, head0)
            if dil == 1:
                dst = pl.ds(pl.multiple_of(t * QT, QT), QT)
            elif dil == STRIDE_STEP:
                dst = pl.ds(r + dil * QT * t, QT, stride=dil)
            else:
                dst = pl.ds((r // STRIDE_STEP) * (SEQ // STRIDE_STEP) + r % STRIDE_STEP, QT, stride=STRIDE_STEP)
            if dil == STRIDE_STEP ** 2:
                stage_ref[0, dst, :] = o
                stage_ref[1, dst, :] = l
                stage_ref[2, dst, :] = m
            else:
                o_ref[g, dst, :] = o
                l_ref[g, dst, :] = l
                m_ref[g, dst, :] = m
            return carry

        lax.fori_loop(0, SEQ // QT, tile, 0, unroll=A_UNROLL)

        if dil == STRIDE_STEP ** 2:
            quarter = SEQ // STRIDE_STEP
            for k, dst_ref in enumerate((o_ref, l_ref, m_ref)):
                for ra in range(STRIDE_STEP):
                    dst_ref[g, pl.ds(ra, quarter, stride=STRIDE_STEP), :] = (
                        stage_ref[k, ra * quarter:(ra + 1) * quarter, :])

    def combine(u, carry):
        sl = pl.ds(pl.multiple_of(u * QT, QT), QT)
        ms = [m_ref[g, sl, :] for g in range(3)]
        mm = jnp.maximum(jnp.maximum(ms[0], ms[1]), ms[2])
        ws = [jnp.exp(m - mm) for m in ms]
        num = ws[0] * o_ref[0, sl, :] + ws[1] * o_ref[1, sl, :] + ws[2] * o_ref[2, sl, :]
        den = ws[0] * l_ref[0, sl, :] + ws[1] * l_ref[1, sl, :] + ws[2] * l_ref[2, sl, :]
        gate = ag_ref[0, sl, :].astype(F32)
        out_ref[0, sl, :] = ((num / den) * _silu(gate)).astype(BF16)
        return carry

    lax.fori_loop(0, SEQ // QT, combine, 0)


def _dilated_mixer(coef, proj_nat, proj_dil):
    b = proj_nat.shape[0]
    grid_spec = pltpu.PrefetchScalarGridSpec(
        num_scalar_prefetch=1,
        grid=(b, N_PAIRS),
        in_specs=[
            pl.BlockSpec((None, SEQ, 3 * LANES), lambda i, j, c: (i, 0, COL_A_QKV // (3 * LANES) + j)),
            pl.BlockSpec((None, None, SEQ, 3 * LANES), lambda i, j, c: (i, 0, 0, j)),
            pl.BlockSpec((None, None, SEQ, 3 * LANES), lambda i, j, c: (i, 1, 0, j)),
            pl.BlockSpec((1, SEQ, LANES), lambda i, j, c: (i, 0, COL_AG // LANES + j)),
        ],
        out_specs=pl.BlockSpec((1, SEQ, LANES), lambda i, j, c: (i, 0, j)),
        scratch_shapes=[
            pltpu.VMEM((SEQ, 2 * LANES), BF16),
            pltpu.VMEM((6, QT, 2 * QT), F32),
            pltpu.VMEM((3, SEQ, LANES), F32),
            pltpu.VMEM((3, SEQ, LANES), F32),
            pltpu.VMEM((3, SEQ, LANES), F32),
            pltpu.VMEM((3, SEQ, LANES), F32),
        ],
    )
    return pl.pallas_call(
        _dilated_kernel,
        grid_spec=grid_spec,
        out_shape=jax.ShapeDtypeStruct((b, SEQ, N_PAIRS * LANES), BF16),
        compiler_params=pltpu.CompilerParams(
            dimension_semantics=("arbitrary", "arbitrary"), vmem_limit_bytes=VMEM_LIMIT),
        name="dilated_mixer",
    )(coef, proj_nat, proj_dil, proj_dil, proj_nat)


def _neighbour_kernel(qkv_ref, bg_ref, tab_ref, out_ref, vext_ref):
    head0 = lax.broadcasted_iota(jnp.int32, (QT, LANES), 1) < HEAD_DIM
    vext_ref[:, LANES:] = jnp.ones((SEQ, LANES), BF16)
    vext_ref[:, :LANES] = qkv_ref[0, :, 2 * LANES:3 * LANES]
    scale = HEAD_DIM ** -0.5
    n_tiles = SEQ // QT

    def tile(t, carry):
        ksr = jnp.clip(2 * t - NA_ROWS // 2, 0, ROWS - B_KROWS)
        krow = pl.multiple_of(ksr * GRID_W, GRID_W)
        vidx = jnp.where(t < 2, t, jnp.where(t > n_tiles - 3, t - (n_tiles - 5), 2))
        sl = pl.ds(pl.multiple_of(t * QT, QT), QT)
        q = qkv_ref[0, sl, 0:LANES] * jnp.asarray(scale, BF16)
        k = qkv_ref[0, pl.ds(krow, B_KEYS), LANES:2 * LANES]
        vext = vext_ref[pl.ds(krow, B_KEYS), :]
        biases = [tab_ref[0, vidx, hh] for hh in range(2)]
        o, l, _ = _pair_attention(q, k, vext, biases, head0)
        gate = bg_ref[0, sl, :].astype(F32)
        out_ref[0, sl, :] = ((o / l) * _silu(gate)).astype(BF16)
        return carry

    lax.fori_loop(0, n_tiles, tile, 0, unroll=B_UNROLL)


def _neighbour_table(rpb):
    wr = min(NA_ROWS, ROWS)
    pad = GRID_W - NA_COLS
    rpb_pad = jnp.pad(rpb, ((0, 0), (0, 0), (pad, pad)))
    toeplitz = jnp.stack([rpb_pad[:, :, GRID_W - 1 - c:2 * GRID_W - 1 - c] for c in range(GRID_W)], axis=2)
    qi = np.arange(QT)
    ki = np.arange(B_KEYS)
    rr, c = qi // GRID_W, qi % GRID_W
    w, kc = ki // GRID_W, ki % GRID_W
    sc = np.clip(c - NA_COLS // 2, 0, GRID_W - NA_COLS)
    col_ok = (kc[None, :] >= sc[:, None]) & (kc[None, :] < sc[:, None] + NA_COLS)
    variants, ok_l = [], []
    for t in B_VARIANTS:
        ks = int(np.clip(2 * t - NA_ROWS // 2, 0, ROWS - B_KROWS))
        halves = []
        for q_row in (2 * t, 2 * t + 1):
            blocks = [toeplitz[:, int(np.clip(ks + kr - q_row + NA_ROWS - 1, 0, 2 * NA_ROWS - 2))]
                      for kr in range(B_KROWS)]
            halves.append(jnp.stack(blocks, axis=2).reshape(A_HEADS, GRID_W, B_KEYS))
        variants.append(jnp.concatenate(halves, axis=1))
        r = 2 * t + rr
        r0 = np.clip(r - wr // 2, 0, ROWS - wr)
        keyrow = ks + w
        row_ok = (keyrow[None, :] >= r0[:, None]) & (keyrow[None, :] < r0[:, None] + wr)
        ok_l.append(row_ok & col_ok)
    tab = jnp.where(np.stack(ok_l)[None], jnp.stack(variants, axis=1), NEG_INF)
    tab = tab.reshape(N_PAIRS, 2, len(B_VARIANTS), QT, B_KEYS)
    return tab.transpose(0, 2, 1, 3, 4)


def _neighbour_mixer(proj_nat, table):
    b = proj_nat.shape[0]
    return pl.pallas_call(
        _neighbour_kernel,
        grid=(N_PAIRS, b),
        in_specs=[
            pl.BlockSpec((1, SEQ, 3 * LANES), lambda j, i: (i, 0, COL_B_QKV // (3 * LANES) + j)),
            pl.BlockSpec((1, SEQ, LANES), lambda j, i: (i, 0, COL_BG // LANES + j)),
            pl.BlockSpec((1, len(B_VARIANTS), 2, QT, B_KEYS), lambda j, i: (j, 0, 0, 0, 0)),
        ],
        out_specs=pl.BlockSpec((1, SEQ, LANES), lambda j, i: (i, 0, j)),
        out_shape=jax.ShapeDtypeStruct((b, SEQ, N_PAIRS * LANES), BF16),
        scratch_shapes=[pltpu.VMEM((SEQ, 2 * LANES), BF16)],
        compiler_params=pltpu.CompilerParams(
            dimension_semantics=("arbitrary", "arbitrary"), vmem_limit_bytes=VMEM_LIMIT),
        name="neighbour_mixer",
    )(proj_nat, proj_nat, table)


def _memory_kernel(q_ref, mg_ref, mk_ref, mv_ref, out_ref, vext_ref):
    vext_ref[:, LANES:] = jnp.ones((N_MEM, LANES), BF16)
    vext_ref[:, :LANES] = mv_ref[0]
    scale = M_HEAD_DIM ** -0.5

    def tile(t, carry):
        sl = pl.ds(pl.multiple_of(t * M_QT, M_QT), M_QT)
        s = _dot_nt(q_ref[0, sl, :], mk_ref[0]) * scale
        m = jnp.max(s, axis=-1, keepdims=True)
        p = jnp.exp(s - m).astype(BF16)
        oe = jnp.dot(p, vext_ref[...], preferred_element_type=F32)
        gate = mg_ref[0, sl, :].astype(F32)
        out_ref[0, sl, :] = ((oe[:, :LANES] / oe[:, LANES:]) * _silu(gate)).astype(BF16)
        return carry

    lax.fori_loop(0, SEQ // M_QT, tile, 0, unroll=M_UNROLL)


def _memory_mixer(proj_nat, mem_kv):
    b = proj_nat.shape[0]
    return pl.pallas_call(
        _memory_kernel,
        grid=(b, M_HEADS),
        in_specs=[
            pl.BlockSpec((1, SEQ, LANES), lambda i, j: (i, 0, COL_MQ // LANES + j)),
            pl.BlockSpec((1, SEQ, LANES), lambda i, j: (i, 0, COL_MG // LANES + j)),
            pl.BlockSpec((1, N_MEM, LANES), lambda i, j: (i, 0, j)),
            pl.BlockSpec((1, N_MEM, LANES), lambda i, j: (i, 0, M_HEADS + j)),
        ],
        out_specs=pl.BlockSpec((1, SEQ, LANES), lambda i, j: (i, 0, j)),
        out_shape=jax.ShapeDtypeStruct((b, SEQ, M_HEADS * LANES), BF16),
        scratch_shapes=[pltpu.VMEM((N_MEM, 2 * LANES), BF16)],
        compiler_params=pltpu.CompilerParams(
            dimension_semantics=("arbitrary", "arbitrary"), vmem_limit_bytes=VMEM_LIMIT),
        name="memory_mixer",
    )(proj_nat, proj_nat, mem_kv, mem_kv)


def _merge_out_kernel(ga_ref, gb_ref, gm_ref, s0_ref, s1_ref, s2_ref, x_ref,
                      wa_ref, wb_ref, wm_ref, wo_ref, g_ref, out_ref):
    merged = None
    for br_ref, w_ref, s_ref in ((ga_ref, wa_ref, s0_ref), (gb_ref, wb_ref, s1_ref),
                                 (gm_ref, wm_ref, s2_ref)):
        branch = jnp.dot(br_ref[0], w_ref[...], preferred_element_type=F32)
        term = jax.nn.sigmoid(s_ref[0].astype(F32)) * branch
        merged = term if merged is None else merged + term
    y = x_ref[0] + jnp.dot(merged.astype(BF16), wo_ref[...], preferred_element_type=F32)
    ms = jnp.mean(y * y, axis=-1, keepdims=True)
    out_ref[0] = (y * lax.rsqrt(ms + RMS_EPS)) * g_ref[...]


def _merge_out(ga, gb, gm, proj_nat, x, wa, wb, wm, wo, gain, *, tm=512):
    b = x.shape[0]
    width = ga.shape[-1]
    br_spec = pl.BlockSpec((1, tm, width), lambda i, j: (i, j, 0))
    w_spec = pl.BlockSpec((width, D_MODEL), lambda i, j: (0, 0))
    gate_specs = [
        pl.BlockSpec((1, tm, D_MODEL), functools.partial(lambda i, j, n: (i, j, COL_MERGE // D_MODEL + n), n=n))
        for n in range(3)
    ]
    return pl.pallas_call(
        _merge_out_kernel,
        grid=(b, SEQ // tm),
        in_specs=[br_spec, br_spec, br_spec, *gate_specs,
                  pl.BlockSpec((1, tm, D_MODEL), lambda i, j: (i, j, 0)),
                  w_spec, w_spec, w_spec,
                  pl.BlockSpec((D_MODEL, D_MODEL), lambda i, j: (0, 0)),
                  pl.BlockSpec((1, D_MODEL), lambda i, j: (0, 0))],
        out_specs=pl.BlockSpec((1, tm, D_MODEL), lambda i, j: (i, j, 0)),
        out_shape=jax.ShapeDtypeStruct((b, SEQ, D_MODEL), F32),
        compiler_params=pltpu.CompilerParams(
            dimension_semantics=("arbitrary", "arbitrary"), vmem_limit_bytes=VMEM_LIMIT),
        name="merge_out",
    )(ga, gb, gm, proj_nat, proj_nat, proj_nat, x, wa, wb, wm, wo, gain.reshape(1, D_MODEL))


def _prepare_weights(w_in):
    a_w = 3 * A_HEADS * HEAD_DIM
    b_w = A_HEADS * HEAD_DIM
    aq, ak, av = (w_in[:, i * a_w:(i + 1) * a_w].reshape(D_MODEL, 3, N_PAIRS, LANES) for i in range(3))
    a_qkv = jnp.stack([aq, ak, av], axis=3)
    off = 3 * a_w
    ag = w_in[:, off:off + b_w]
    off += b_w
    bq, bk, bv = (w_in[:, off + i * b_w:off + (i + 1) * b_w].reshape(D_MODEL, N_PAIRS, LANES) for i in range(3))
    b_qkv = jnp.stack([bq, bk, bv], axis=2).reshape(D_MODEL, 3 * b_w)
    off += 3 * b_w
    bg = w_in[:, off:off + b_w]
    rest = w_in[:, off + b_w:]
    w_nat = jnp.concatenate([a_qkv[:, 0].reshape(D_MODEL, a_w), b_qkv, ag, bg, rest], axis=1)
    w_dil = a_qkv[:, 1:].reshape(D_MODEL, 2 * N_DIL_STEPS, DIL_TN).transpose(1, 0, 2)
    return w_nat.astype(BF16), w_dil.astype(BF16)


def _alibi_coefficients():
    n_heads = len(A_GROUPS) * A_HEADS
    slopes = jnp.exp2(-8.0 * jnp.arange(1, n_heads + 1, dtype=F32) / n_heads)
    dils = np.repeat(np.array([d for _, d in A_GROUPS], np.float32), A_HEADS)
    return slopes * dils


def _trunk(x, mem, weights):
    (g_norm, g_mem, w_nat, w_dil, w_mem, table, coef, wa, wb, wm, wo, g_final) = weights
    proj_nat, proj_dil = _fused_proj(x, g_norm, w_nat, w_dil)
    mem_kv = _norm_proj(mem, g_mem, w_mem, dil=1, tn=1024)
    ga = _dilated_mixer(coef, proj_nat, proj_dil)
    gb = _neighbour_mixer(proj_nat, table)
    gm = _memory_mixer(proj_nat, mem_kv)
    return _merge_out(ga, gb, gm, proj_nat, x, wa, wb, wm, wo, g_final)


def kernel(x_prompt, x_sample, mem_prompt, mem_sample, norm_gain, mem_norm_gain, w_in, w_mem_kv,
           rpb, w_proj_a, w_proj_b, w_proj_m, w_out, final_norm_gain):
    assert norm_gain.shape[0] == 1, "single-layer trunk"
    w_nat, w_dil = _prepare_weights(w_in[0])
    weights = (norm_gain[0], mem_norm_gain[0], w_nat, w_dil, w_mem_kv[0].astype(BF16),
               _neighbour_table(rpb[0]), _alibi_coefficients(),
               w_proj_a[0].astype(BF16), w_proj_b[0].astype(BF16), w_proj_m[0].astype(BF16),
               w_out[0].astype(BF16), final_norm_gain)
    return (_trunk(x_prompt, mem_prompt, weights), _trunk(x_sample, mem_sample, weights))
```

```python
import functools

import numpy as np
import jax
import jax.numpy as jnp
from jax import lax
from jax.experimental import pallas as pl
from jax.experimental.pallas import tpu as pltpu

F32 = jnp.float32
BF16 = jnp.bfloat16

D_MODEL = 1024
SEQ = 2048
N_MEM = 256
GRID_W = 64
ROWS = SEQ // GRID_W
HEAD_DIM = 64
A_GROUPS = ((128, 1), (512, 4), (2048, 16))
A_HEADS = 8
NA_ROWS = 8
NA_COLS = 16
M_HEADS = 4
M_HEAD_DIM = 128
RMS_EPS = 1e-6
NEG_INF = -1e30

LANES = 128
N_PAIRS = A_HEADS // 2
QT = 128
STRIDE_STEP = 4
A_REACH = 64
B_KEYS = min(NA_ROWS, ROWS) * GRID_W
M_QT = 256
A_UNROLL = 8
B_UNROLL = 32
M_UNROLL = 8

NAT_COLS = 8192
COL_A_QKV = 0
COL_B_QKV = 1536
COL_AG = 3072
COL_BG = 3584
COL_MQ = 4096
COL_MG = 4608
COL_MERGE = 5120
DIL_COLS = 1536

VMEM_LIMIT = 52 * 1024 * 1024
PROJ_VMEM_LIMIT = 56 * 1024 * 1024


def _silu(x):
    return x * jax.nn.sigmoid(x)


def _dot_nt(a, b):
    return lax.dot_general(a, b, (((1,), (1,)), ((), ())), preferred_element_type=F32)


def _norm_proj_kernel(x_ref, g_ref, w_ref, o_ref, h_ref, *, dil, seq):
    sub = seq // dil
    rows = min(sub, 256)

    @pl.when(pl.program_id(1) == 0)
    def _():
        g = g_ref[...]
        for r in range(dil):
            for c in range(sub // rows):
                xs = x_ref[0, c * rows:(c + 1) * rows, r * D_MODEL:(r + 1) * D_MODEL]
                ms = jnp.mean(xs * xs, axis=-1, keepdims=True)
                h = (xs * lax.rsqrt(ms + RMS_EPS)) * g
                h_ref[r * sub + c * rows:r * sub + (c + 1) * rows, :] = h.astype(BF16)

    mt = min(seq, 512)
    for c in range(seq // mt):
        acc = jnp.dot(h_ref[c * mt:(c + 1) * mt, :], w_ref[...], preferred_element_type=F32)
        o_ref[0, c * mt:(c + 1) * mt, :] = acc.astype(BF16)


def _norm_proj(x, gain, w, *, dil, tn):
    b, seq, _ = x.shape
    n = w.shape[1]
    sub = seq // dil
    xv = x.reshape(b, sub, dil * D_MODEL)
    return pl.pallas_call(
        functools.partial(_norm_proj_kernel, dil=dil, seq=seq),
        grid=(b, n // tn),
        in_specs=[
            pl.BlockSpec((1, sub, dil * D_MODEL), lambda i, j: (i, 0, 0)),
            pl.BlockSpec((1, D_MODEL), lambda i, j: (0, 0)),
            pl.BlockSpec((D_MODEL, tn), lambda i, j: (0, j)),
        ],
        out_specs=pl.BlockSpec((1, seq, tn), lambda i, j: (i, 0, j)),
        out_shape=jax.ShapeDtypeStruct((b, seq, n), BF16),
        scratch_shapes=[pltpu.VMEM((seq, D_MODEL), BF16)],
        compiler_params=pltpu.CompilerParams(
            dimension_semantics=("arbitrary", "arbitrary"), vmem_limit_bytes=VMEM_LIMIT),
        name=f"norm_proj_d{dil}",
    )(xv, gain.reshape(1, D_MODEL), w)


NAT_TN = 1024
N_NAT_STEPS = NAT_COLS // NAT_TN
DIL_TN = 768
N_DIL_STEPS = DIL_COLS // DIL_TN
PROJ_MT = 512


def _fused_proj_kernel(x_ref, g_ref, wn_ref, wd_ref, on_ref, od_ref,
                       hn_ref, h4_ref, h16_ref, inv_ref, sa_ref, sb_ref):
    assert N_NAT_STEPS == D_MODEL // LANES, "one lane slab is regrouped per natural-order column step"
    j = pl.program_id(1)
    quarter = SEQ // STRIDE_STEP

    @pl.when(j == 0)
    def _():
        g = g_ref[...]
        rows = 256
        for c in range(SEQ // rows):
            sl = slice(c * rows, (c + 1) * rows)
            xs = x_ref[0, sl, :]
            inv = lax.rsqrt(jnp.mean(xs * xs, axis=-1, keepdims=True) + RMS_EPS)
            inv_ref[sl, :] = jnp.broadcast_to(inv, (rows, LANES))
            hn_ref[sl, :] = ((xs * inv) * g).astype(BF16)

    @pl.when(j < N_NAT_STEPS)
    def _():
        cols = pl.ds(pl.multiple_of(j * LANES, LANES), LANES)
        sa_ref[...] = (x_ref[0, :, cols] * inv_ref[...]) * g_ref[:, cols]
        for ra in range(STRIDE_STEP):
            part = sa_ref[pl.ds(ra, quarter, stride=STRIDE_STEP), :]
            sb_ref[ra * quarter:(ra + 1) * quarter, :] = part
            h4_ref[ra * quarter:(ra + 1) * quarter, cols] = part.astype(BF16)
        for slot in range(STRIDE_STEP ** 2):
            ra, rb = slot // STRIDE_STEP, slot % STRIDE_STEP
            h16_ref[slot * QT:(slot + 1) * QT, cols] = (
                sb_ref[pl.ds(ra * quarter + rb, QT, stride=STRIDE_STEP), :].astype(BF16))
        for c in range(SEQ // PROJ_MT):
            sl = slice(c * PROJ_MT, (c + 1) * PROJ_MT)
            on_ref[0, sl, :] = jnp.dot(hn_ref[sl, :], wn_ref[...], preferred_element_type=F32).astype(BF16)

    for group, h_ref in enumerate((h4_ref, h16_ref)):
        first = N_NAT_STEPS + group * N_DIL_STEPS

        @pl.when((j >= first) & (j < first + N_DIL_STEPS))
        def _(h_ref=h_ref):
            for c in range(SEQ // PROJ_MT):
                sl = slice(c * PROJ_MT, (c + 1) * PROJ_MT)
                od_ref[0, 0, sl, :] = jnp.dot(h_ref[sl, :], wd_ref[0], preferred_element_type=F32).astype(BF16)


def _fused_proj(x, gain, w_nat, w_dil):
    b = x.shape[0]

    def dil_step(j):
        return jnp.clip(j - N_NAT_STEPS, 0, 2 * N_DIL_STEPS - 1)

    return pl.pallas_call(
        _fused_proj_kernel,
        grid=(b, N_NAT_STEPS + 2 * N_DIL_STEPS),
        in_specs=[
            pl.BlockSpec((1, SEQ, D_MODEL), lambda i, j: (i, 0, 0)),
            pl.BlockSpec((1, D_MODEL), lambda i, j: (0, 0)),
            pl.BlockSpec((D_MODEL, NAT_TN), lambda i, j: (0, jnp.minimum(j, N_NAT_STEPS - 1))),
            pl.BlockSpec((1, D_MODEL, DIL_TN), lambda i, j: (dil_step(j), 0, 0)),
        ],
        out_specs=[
            pl.BlockSpec((1, SEQ, NAT_TN), lambda i, j: (i, 0, jnp.minimum(j, N_NAT_STEPS - 1))),
            pl.BlockSpec((1, 1, SEQ, DIL_TN),
                         lambda i, j: (i, dil_step(j) // N_DIL_STEPS, 0, dil_step(j) % N_DIL_STEPS)),
        ],
        out_shape=[jax.ShapeDtypeStruct((b, SEQ, NAT_COLS), BF16),
                   jax.ShapeDtypeStruct((b, 2, SEQ, DIL_COLS), BF16)],
        scratch_shapes=[
            pltpu.VMEM((SEQ, D_MODEL), BF16),
            pltpu.VMEM((SEQ, D_MODEL), BF16),
            pltpu.VMEM((SEQ, D_MODEL), BF16),
            pltpu.VMEM((SEQ, LANES), F32),
            pltpu.VMEM((SEQ, LANES), F32),
            pltpu.VMEM((SEQ, LANES), F32),
        ],
        compiler_params=pltpu.CompilerParams(
            dimension_semantics=("arbitrary", "arbitrary"), vmem_limit_bytes=PROJ_VMEM_LIMIT),
        name="fused_proj",
    )(x, gain.reshape(1, D_MODEL), w_nat, w_dil)


def _pair_attention(q, k, vext, bias):
    rows = q.shape[0]
    head0 = lax.broadcasted_iota(jnp.int32, (rows, LANES), 1) < HEAD_DIM
    zero = jnp.zeros_like(q)
    q2 = jnp.concatenate([jnp.where(head0, q, zero), jnp.where(head0, zero, q)], axis=0)
    s = _dot_nt(q2, k) + bias
    m = jnp.max(s, axis=-1, keepdims=True)
    p = jnp.exp(s - m).astype(BF16)
    oe = jnp.dot(p, vext, preferred_element_type=F32)
    o = jnp.where(head0, oe[:rows, :LANES], oe[rows:, :LANES])
    l = jnp.where(head0, oe[:rows, LANES:], oe[rows:, LANES:])
    m = jnp.where(head0, m[:rows], m[rows:])
    return o, l, m


def _dilated_kernel(coef_ref, q0_ref, q1_ref, q2_ref, ag_ref, out_ref,
                    vext_ref, bias_ref, o_ref, l_ref, m_ref, stage_ref):
    hp = pl.program_id(1)
    scale = HEAD_DIM ** -0.5
    tk = 2 * QT
    n_tiles = SEQ // QT
    quarter = SEQ // STRIDE_STEP
    qkv_refs = (q0_ref, q1_ref, q2_ref)
    g_staged = [d for _, d in A_GROUPS].index(STRIDE_STEP ** 2)

    row = lax.broadcasted_iota(jnp.int32, (QT, tk), 0)
    col = lax.broadcasted_iota(jnp.int32, (QT, tk), 1)
    for g, ((_, dil), qkv_ref) in enumerate(zip(A_GROUPS, qkv_refs)):
        vext_ref[g, :, LANES:] = jnp.ones((SEQ, LANES), BF16)
        vext_ref[g, :, :LANES] = qkv_ref[:, 2 * LANES:3 * LANES]
        if SEQ // dil > QT:
            variants = ((0, 0, tk), (-A_REACH, 0, tk), (-2 * A_REACH, 0, tk))
        else:
            variants = ((0, 0, QT), (-QT, QT, tk))
        for v, (off, lo, hi) in enumerate(variants):
            rel = jnp.abs(col + off - row)
            relf = rel.astype(F32)
            masked = (rel > A_REACH) | (col < lo) | (col >= hi)
            for hh in range(2):
                c = coef_ref[g * A_HEADS + 2 * hp + hh]
                bias_ref[g, v, hh * QT:(hh + 1) * QT, :] = jnp.where(masked, NEG_INF, -c * relf)

    def group_tile(u, g):
        dil = A_GROUPS[g][1]
        qkv_ref = qkv_refs[g]
        sub = SEQ // dil
        nt = sub // QT
        if nt == 1:
            r, t, vidx = u, 0, u % 2
            ks = (u % 2) * -QT
        else:
            r, t = u // nt, u % nt
            ks = jnp.clip(t * QT - A_REACH, 0, sub - tk)
            vidx = jnp.where(t == 0, 0, jnp.where(t == nt - 1, 2, 1))
        qrow = pl.multiple_of(r * sub + t * QT, QT)
        krow = pl.multiple_of(r * sub + ks, A_REACH)
        q = qkv_ref[pl.ds(qrow, QT), 0:LANES] * jnp.asarray(scale, BF16)
        k = qkv_ref[pl.ds(krow, tk), LANES:2 * LANES]
        vext = vext_ref[g, pl.ds(krow, tk), :]
        o, l, m = _pair_attention(q, k, vext, bias_ref[g, vidx])
        if dil == 1:
            dst = pl.ds(pl.multiple_of(t * QT, QT), QT)
        elif dil == STRIDE_STEP:
            dst = pl.ds(r + dil * QT * t, QT, stride=dil)
        else:
            dst = pl.ds((r // STRIDE_STEP) * quarter + r % STRIDE_STEP, QT, stride=STRIDE_STEP)
        if dil == STRIDE_STEP ** 2:
            stage_ref[0, dst, :] = o
            stage_ref[1, dst, :] = l
            stage_ref[2, dst, :] = m
        else:
            o_ref[g, dst, :] = o
            l_ref[g, dst, :] = l
            m_ref[g, dst, :] = m

    def tiles(u, carry):
        for g in range(len(A_GROUPS)):
            group_tile(u, g)
        return carry

    lax.fori_loop(0, n_tiles, tiles, 0, unroll=A_UNROLL)

    def interleave(ra, carry):
        src = pl.ds(pl.multiple_of(ra * quarter, quarter), quarter)
        for k, dst_ref in enumerate((o_ref, l_ref, m_ref)):
            dst_ref[g_staged, pl.ds(ra, quarter, stride=STRIDE_STEP), :] = stage_ref[k, src, :]
        return carry

    lax.fori_loop(0, STRIDE_STEP, interleave, 0)

    def combine(u, carry):
        sl = pl.ds(pl.multiple_of(u * QT, QT), QT)
        ms = [m_ref[g, sl, :] for g in range(3)]
        mm = jnp.maximum(jnp.maximum(ms[0], ms[1]), ms[2])
        ws = [jnp.exp(m - mm) for m in ms]
        num = ws[0] * o_ref[0, sl, :] + ws[1] * o_ref[1, sl, :] + ws[2] * o_ref[2, sl, :]
        den = ws[0] * l_ref[0, sl, :] + ws[1] * l_ref[1, sl, :] + ws[2] * l_ref[2, sl, :]
        gate = ag_ref[0, sl, :].astype(F32)
        out_ref[0, sl, :] = ((num / den) * _silu(gate)).astype(BF16)
        return carry

    lax.fori_loop(0, SEQ // QT, combine, 0)


def _dilated_mixer(coef, proj_nat, proj_dil):
    b = proj_nat.shape[0]
    grid_spec = pltpu.PrefetchScalarGridSpec(
        num_scalar_prefetch=1,
        grid=(b, N_PAIRS),
        in_specs=[
            pl.BlockSpec((None, SEQ, 3 * LANES), lambda i, j, c: (i, 0, COL_A_QKV // (3 * LANES) + j)),
            pl.BlockSpec((None, None, SEQ, 3 * LANES), lambda i, j, c: (i, 0, 0, j)),
            pl.BlockSpec((None, None, SEQ, 3 * LANES), lambda i, j, c: (i, 1, 0, j)),
            pl.BlockSpec((1, SEQ, LANES), lambda i, j, c: (i, 0, COL_AG // LANES + j)),
        ],
        out_specs=pl.BlockSpec((1, SEQ, LANES), lambda i, j, c: (i, 0, j)),
        scratch_shapes=[
            pltpu.VMEM((3, SEQ, 2 * LANES), BF16),
            pltpu.VMEM((3, 3, 2 * QT, 2 * QT), F32),
            pltpu.VMEM((3, SEQ, LANES), F32),
            pltpu.VMEM((3, SEQ, LANES), F32),
            pltpu.VMEM((3, SEQ, LANES), F32),
            pltpu.VMEM((3, SEQ, LANES), F32),
        ],
    )
    return pl.pallas_call(
        _dilated_kernel,
        grid_spec=grid_spec,
        out_shape=jax.ShapeDtypeStruct((b, SEQ, N_PAIRS * LANES), BF16),
        compiler_params=pltpu.CompilerParams(
            dimension_semantics=("arbitrary", "arbitrary"), vmem_limit_bytes=VMEM_LIMIT),
        name="dilated_mixer",
    )(coef, proj_nat, proj_dil, proj_dil, proj_nat)


def _neighbour_kernel(qkv_ref, bg_ref, tab_ref, out_ref, vext_ref):
    vext_ref[:, LANES:] = jnp.ones((SEQ, LANES), BF16)
    vext_ref[:, :LANES] = qkv_ref[0, :, 2 * LANES:3 * LANES]
    scale = HEAD_DIM ** -0.5
    wr = min(NA_ROWS, ROWS)

    def tile(r, carry):
        r0 = jnp.clip(r - wr // 2, 0, ROWS - wr)
        krow = pl.multiple_of(r0 * GRID_W, GRID_W)
        sl = pl.ds(pl.multiple_of(r * GRID_W, GRID_W), GRID_W)
        q = qkv_ref[0, sl, 0:LANES] * jnp.asarray(scale, BF16)
        k = qkv_ref[0, pl.ds(krow, B_KEYS), LANES:2 * LANES]
        vext = vext_ref[pl.ds(krow, B_KEYS), :]
        o, l, _ = _pair_attention(q, k, vext, tab_ref[0, r - r0])
        gate = bg_ref[0, sl, :].astype(F32)
        out_ref[0, sl, :] = ((o / l) * _silu(gate)).astype(BF16)
        return carry

    lax.fori_loop(0, ROWS, tile, 0, unroll=B_UNROLL)


def _neighbour_table(rpb):
    wr = min(NA_ROWS, ROWS)
    pad = GRID_W - NA_COLS
    rpb_pad = jnp.pad(rpb, ((0, 0), (0, 0), (pad, pad)))
    toeplitz = jnp.stack([rpb_pad[:, :, GRID_W - 1 - c:2 * GRID_W - 1 - c] for c in range(GRID_W)], axis=2)
    variants = []
    for d in range(wr):
        blocks = [toeplitz[:, w - d + NA_ROWS - 1] for w in range(wr)]
        variants.append(jnp.stack(blocks, axis=2).reshape(A_HEADS, GRID_W, B_KEYS))
    tab = jnp.stack(variants, axis=1)
    c = np.arange(GRID_W)
    kc = np.arange(B_KEYS) % GRID_W
    sc = np.clip(c - NA_COLS // 2, 0, GRID_W - NA_COLS)
    col_ok = (kc[None, :] >= sc[:, None]) & (kc[None, :] < sc[:, None] + NA_COLS)
    tab = jnp.where(col_ok[None, None], tab, NEG_INF)
    tab = tab.reshape(N_PAIRS, 2, wr, GRID_W, B_KEYS).transpose(0, 2, 1, 3, 4)
    return tab.reshape(N_PAIRS, wr, 2 * GRID_W, B_KEYS)


def _neighbour_mixer(proj_nat, table):
    b = proj_nat.shape[0]
    return pl.pallas_call(
        _neighbour_kernel,
        grid=(N_PAIRS, b),
        in_specs=[
            pl.BlockSpec((1, SEQ, 3 * LANES), lambda j, i: (i, 0, COL_B_QKV // (3 * LANES) + j)),
            pl.BlockSpec((1, SEQ, LANES), lambda j, i: (i, 0, COL_BG // LANES + j)),
            pl.BlockSpec((1, min(NA_ROWS, ROWS), 2 * GRID_W, B_KEYS), lambda j, i: (j, 0, 0, 0)),
        ],
        out_specs=pl.BlockSpec((1, SEQ, LANES), lambda j, i: (i, 0, j)),
        out_shape=jax.ShapeDtypeStruct((b, SEQ, N_PAIRS * LANES), BF16),
        scratch_shapes=[pltpu.VMEM((SEQ, 2 * LANES), BF16)],
        compiler_params=pltpu.CompilerParams(
            dimension_semantics=("arbitrary", "arbitrary"), vmem_limit_bytes=VMEM_LIMIT),
        name="neighbour_mixer",
    )(proj_nat, proj_nat, table)


def _memory_kernel(q_ref, mg_ref, mk_ref, mv_ref, out_ref, vext_ref):
    vext_ref[:, LANES:] = jnp.ones((N_MEM, LANES), BF16)
    vext_ref[:, :LANES] = mv_ref[0]
    scale = M_HEAD_DIM ** -0.5

    def tile(t, carry):
        sl = pl.ds(pl.multiple_of(t * M_QT, M_QT), M_QT)
        s = _dot_nt(q_ref[0, sl, :], mk_ref[0]) * scale
        m = jnp.max(s, axis=-1, keepdims=True)
        p = jnp.exp(s - m).astype(BF16)
        oe = jnp.dot(p, vext_ref[...], preferred_element_type=F32)
        gate = mg_ref[0, sl, :].astype(F32)
        out_ref[0, sl, :] = ((oe[:, :LANES] / oe[:, LANES:]) * _silu(gate)).astype(BF16)
        return carry

    lax.fori_loop(0, SEQ // M_QT, tile, 0, unroll=M_UNROLL)


def _memory_mixer(proj_nat, mem_kv):
    b = proj_nat.shape[0]
    return pl.pallas_call(
        _memory_kernel,
        grid=(b, M_HEADS),
        in_specs=[
            pl.BlockSpec((1, SEQ, LANES), lambda i, j: (i, 0, COL_MQ // LANES + j)),
            pl.BlockSpec((1, SEQ, LANES), lambda i, j: (i, 0, COL_MG // LANES + j)),
            pl.BlockSpec((1, N_MEM, LANES), lambda i, j: (i, 0, j)),
            pl.BlockSpec((1, N_MEM, LANES), lambda i, j: (i, 0, M_HEADS + j)),
        ],
        out_specs=pl.BlockSpec((1, SEQ, LANES), lambda i, j: (i, 0, j)),
        out_shape=jax.ShapeDtypeStruct((b, SEQ, M_HEADS * LANES), BF16),
        scratch_shapes=[pltpu.VMEM((N_MEM, 2 * LANES), BF16)],
        compiler_params=pltpu.CompilerParams(
            dimension_semantics=("arbitrary", "arbitrary"), vmem_limit_bytes=VMEM_LIMIT),
        name="memory_mixer",
    )(proj_nat, proj_nat, mem_kv, mem_kv)


def _merge_out_kernel(ga_ref, gb_ref, gm_ref, s0_ref, s1_ref, s2_ref, x_ref,
                      wa_ref, wb_ref, wm_ref, wo_ref, g_ref, out_ref):
    merged = None
    for br_ref, w_ref, s_ref in ((ga_ref, wa_ref, s0_ref), (gb_ref, wb_ref, s1_ref),
                                 (gm_ref, wm_ref, s2_ref)):
        branch = jnp.dot(br_ref[0], w_ref[...], preferred_element_type=F32)
        term = jax.nn.sigmoid(s_ref[0].astype(F32)) * branch
        merged = term if merged is None else merged + term
    y = x_ref[0] + jnp.dot(merged.astype(BF16), wo_ref[...], preferred_element_type=F32)
    ms = jnp.mean(y * y, axis=-1, keepdims=True)
    out_ref[0] = (y * lax.rsqrt(ms + RMS_EPS)) * g_ref[...]


def _merge_out(ga, gb, gm, proj_nat, x, wa, wb, wm, wo, gain, *, tm=512):
    b = x.shape[0]
    width = ga.shape[-1]
    br_spec = pl.BlockSpec((1, tm, width), lambda i, j: (i, j, 0))
    w_spec = pl.BlockSpec((width, D_MODEL), lambda i, j: (0, 0))
    gate_specs = [
        pl.BlockSpec((1, tm, D_MODEL), functools.partial(lambda i, j, n: (i, j, COL_MERGE // D_MODEL + n), n=n))
        for n in range(3)
    ]
    return pl.pallas_call(
        _merge_out_kernel,
        grid=(b, SEQ // tm),
        in_specs=[br_spec, br_spec, br_spec, *gate_specs,
                  pl.BlockSpec((1, tm, D_MODEL), lambda i, j: (i, j, 0)),
                  w_spec, w_spec, w_spec,
                  pl.BlockSpec((D_MODEL, D_MODEL), lambda i, j: (0, 0)),
                  pl.BlockSpec((1, D_MODEL), lambda i, j: (0, 0))],
        out_specs=pl.BlockSpec((1, tm, D_MODEL), lambda i, j: (i, j, 0)),
        out_shape=jax.ShapeDtypeStruct((b, SEQ, D_MODEL), F32),
        compiler_params=pltpu.CompilerParams(
            dimension_semantics=("arbitrary", "arbitrary"), vmem_limit_bytes=VMEM_LIMIT),
        name="merge_out",
    )(ga, gb, gm, proj_nat, proj_nat, proj_nat, x, wa, wb, wm, wo, gain.reshape(1, D_MODEL))


def _prepare_weights(w_in):
    a_w = 3 * A_HEADS * HEAD_DIM
    b_w = A_HEADS * HEAD_DIM
    aq, ak, av = (w_in[:, i * a_w:(i + 1) * a_w].reshape(D_MODEL, 3, N_PAIRS, LANES) for i in range(3))
    a_qkv = jnp.stack([aq, ak, av], axis=3)
    off = 3 * a_w
    ag = w_in[:, off:off + b_w]
    off += b_w
    bq, bk, bv = (w_in[:, off + i * b_w:off + (i + 1) * b_w].reshape(D_MODEL, N_PAIRS, LANES) for i in range(3))
    b_qkv = jnp.stack([bq, bk, bv], axis=2).reshape(D_MODEL, 3 * b_w)
    off += 3 * b_w
    bg = w_in[:, off:off + b_w]
    rest = w_in[:, off + b_w:]
    w_nat = jnp.concatenate([a_qkv[:, 0].reshape(D_MODEL, a_w), b_qkv, ag, bg, rest], axis=1)
    w_dil = a_qkv[:, 1:].reshape(D_MODEL, 2 * N_DIL_STEPS, DIL_TN).transpose(1, 0, 2)
    return w_nat.astype(BF16), w_dil.astype(BF16)


def _alibi_coefficients():
    n_heads = len(A_GROUPS) * A_HEADS
    slopes = jnp.exp2(-8.0 * jnp.arange(1, n_heads + 1, dtype=F32) / n_heads)
    dils = np.repeat(np.array([d for _, d in A_GROUPS], np.float32), A_HEADS)
    return slopes * dils


def _trunk(x, mem, weights):
    (g_norm, g_mem, w_nat, w_dil, w_mem, table, coef, wa, wb, wm, wo, g_final) = weights
    proj_nat, proj_dil = _fused_proj(x, g_norm, w_nat, w_dil)
    mem_kv = _norm_proj(mem, g_mem, w_mem, dil=1, tn=1024)
    ga = _dilated_mixer(coef, proj_nat, proj_dil)
    gb = _neighbour_mixer(proj_nat, table)
    gm = _memory_mixer(proj_nat, mem_kv)
    return _merge_out(ga, gb, gm, proj_nat, x, wa, wb, wm, wo, g_final)


def kernel(x_prompt, x_sample, mem_prompt, mem_sample, norm_gain, mem_norm_gain, w_in, w_mem_kv,
           rpb, w_proj_a, w_proj_b, w_proj_m, w_out, final_norm_gain):
    assert norm_gain.shape[0] == 1, "single-layer trunk"
    w_nat, w_dil = _prepare_weights(w_in[0])
    weights = (norm_gain[0], mem_norm_gain[0], w_nat, w_dil, w_mem_kv[0].astype(BF16),
               _neighbour_table(rpb[0]), _alibi_coefficients(),
               w_proj_a[0].astype(BF16), w_proj_b[0].astype(BF16), w_proj_m[0].astype(BF16),
               w_out[0].astype(BF16), final_norm_gain)
    return (_trunk(x_prompt, mem_prompt, weights), _trunk(x_sample, mem_sample, weights))
```

```python
import functools

import numpy as np
import jax
import jax.numpy as jnp
from jax import lax
from jax.experimental import pallas as pl
from jax.experimental.pallas import tpu as pltpu

F32 = jnp.float32
BF16 = jnp.bfloat16

D_MODEL = 1024
SEQ = 2048
N_MEM = 256
GRID_W = 64
ROWS = SEQ // GRID_W
HEAD_DIM = 64
A_GROUPS = ((128, 1), (512, 4), (2048, 16))
A_HEADS = 8
NA_ROWS = 8
NA_COLS = 16
M_HEADS = 4
M_HEAD_DIM = 128
RMS_EPS = 1e-6
NEG_INF = -1e30

LANES = 128
N_PAIRS = A_HEADS // 2
QT = 128
STRIDE_STEP = 4
A_REACH = 64
B_KEYS = min(NA_ROWS, ROWS) * GRID_W
M_QT = 256
A_UNROLL = 8
B_UNROLL = 32
M_UNROLL = 8

NAT_COLS = 8192
COL_A_QKV = 0
COL_B_QKV = 1536
COL_AG = 3072
COL_BG = 3584
COL_MQ = 4096
COL_MG = 4608
COL_MERGE = 5120
DIL_COLS = 1536

VMEM_LIMIT = 52 * 1024 * 1024
PROJ_VMEM_LIMIT = 56 * 1024 * 1024


def _gated_ratio(num, den, gate):
    return (num * gate) / (den * (1.0 + jnp.exp(-gate)))


def _dot_nt(a, b):
    return lax.dot_general(a, b, (((1,), (1,)), ((), ())), preferred_element_type=F32)


def _norm_proj_kernel(x_ref, g_ref, w_ref, o_ref, h_ref, *, dil, seq):
    sub = seq // dil
    rows = min(sub, 256)

    @pl.when(pl.program_id(1) == 0)
    def _():
        g = g_ref[...]
        for r in range(dil):
            for c in range(sub // rows):
                xs = x_ref[0, c * rows:(c + 1) * rows, r * D_MODEL:(r + 1) * D_MODEL]
                ms = jnp.mean(xs * xs, axis=-1, keepdims=True)
                h = (xs * lax.rsqrt(ms + RMS_EPS)) * g
                h_ref[r * sub + c * rows:r * sub + (c + 1) * rows, :] = h.astype(BF16)

    mt = min(seq, 512)
    for c in range(seq // mt):
        acc = jnp.dot(h_ref[c * mt:(c + 1) * mt, :], w_ref[...], preferred_element_type=F32)
        o_ref[0, c * mt:(c + 1) * mt, :] = acc.astype(BF16)


def _norm_proj(x, gain, w, *, dil, tn):
    b, seq, _ = x.shape
    n = w.shape[1]
    sub = seq // dil
    xv = x.reshape(b, sub, dil * D_MODEL)
    return pl.pallas_call(
        functools.partial(_norm_proj_kernel, dil=dil, seq=seq),
        grid=(b, n // tn),
        in_specs=[
            pl.BlockSpec((1, sub, dil * D_MODEL), lambda i, j: (i, 0, 0)),
            pl.BlockSpec((1, D_MODEL), lambda i, j: (0, 0)),
            pl.BlockSpec((D_MODEL, tn), lambda i, j: (0, j)),
        ],
        out_specs=pl.BlockSpec((1, seq, tn), lambda i, j: (i, 0, j)),
        out_shape=jax.ShapeDtypeStruct((b, seq, n), BF16),
        scratch_shapes=[pltpu.VMEM((seq, D_MODEL), BF16)],
        compiler_params=pltpu.CompilerParams(
            dimension_semantics=("arbitrary", "arbitrary"), vmem_limit_bytes=VMEM_LIMIT),
        name=f"norm_proj_d{dil}",
    )(xv, gain.reshape(1, D_MODEL), w)


NAT_TN = 1024
N_NAT_STEPS = NAT_COLS // NAT_TN
DIL_TN = 768
N_DIL_STEPS = DIL_COLS // DIL_TN
PROJ_MT = 512


def _fused_proj_kernel(x_ref, xs_ref, g_ref, gs_ref, wn_ref, wd_ref, on_ref, od_ref,
                       hn_ref, h4_ref, h16_ref, inv_ref, sa_ref, sb_ref):
    assert N_NAT_STEPS == D_MODEL // LANES, "one lane slab is regrouped per natural-order column step"
    j = pl.program_id(1)
    quarter = SEQ // STRIDE_STEP

    @pl.when(j == 0)
    def _():
        g = g_ref[...]
        rows = 256
        for c in range(SEQ // rows):
            sl = slice(c * rows, (c + 1) * rows)
            xs = x_ref[0, sl, :]
            inv = lax.rsqrt(jnp.mean(xs * xs, axis=-1, keepdims=True) + RMS_EPS)
            inv_ref[sl, :] = jnp.broadcast_to(inv, (rows, LANES))
            hn_ref[sl, :] = ((xs * inv) * g).astype(BF16)

    @pl.when(j < N_NAT_STEPS)
    def _():
        assert SEQ // PROJ_MT == STRIDE_STEP
        sa_ref[...] = (xs_ref[0] * inv_ref[...]) * gs_ref[...]
        for c in range(SEQ // PROJ_MT):
            sl = slice(c * PROJ_MT, (c + 1) * PROJ_MT)
            on_ref[0, sl, :] = jnp.dot(hn_ref[sl, :], wn_ref[...], preferred_element_type=F32).astype(BF16)
            part = sa_ref[pl.ds(c, quarter, stride=STRIDE_STEP), :]
            sb_ref[c * quarter:(c + 1) * quarter, :] = part
            h4_ref[j, c * quarter:(c + 1) * quarter, :] = part.astype(BF16)
            for rb in range(STRIDE_STEP):
                slot = c * STRIDE_STEP + rb
                h16_ref[j, slot * QT:(slot + 1) * QT, :] = (
                    sb_ref[pl.ds(c * quarter + rb, QT, stride=STRIDE_STEP), :].astype(BF16))

    for group, h_ref in enumerate((h4_ref, h16_ref)):
        first = N_NAT_STEPS + group * N_DIL_STEPS

        @pl.when((j >= first) & (j < first + N_DIL_STEPS))
        def _(h_ref=h_ref):
            for c in range(SEQ // PROJ_MT):
                sl = slice(c * PROJ_MT, (c + 1) * PROJ_MT)
                h = jnp.concatenate([h_ref[k, sl, :] for k in range(N_NAT_STEPS)], axis=1)
                od_ref[0, 0, sl, :] = jnp.dot(h, wd_ref[0], preferred_element_type=F32).astype(BF16)


def _fused_proj(x, gain, w_nat, w_dil):
    b = x.shape[0]

    def dil_step(j):
        return jnp.clip(j - N_NAT_STEPS, 0, 2 * N_DIL_STEPS - 1)

    return pl.pallas_call(
        _fused_proj_kernel,
        grid=(b, N_NAT_STEPS + 2 * N_DIL_STEPS),
        in_specs=[
            pl.BlockSpec((1, SEQ, D_MODEL), lambda i, j: (i, 0, 0)),
            pl.BlockSpec((1, SEQ, LANES), lambda i, j: (i, 0, jnp.minimum(j, N_NAT_STEPS - 1))),
            pl.BlockSpec((1, D_MODEL), lambda i, j: (0, 0)),
            pl.BlockSpec((1, LANES), lambda i, j: (0, jnp.minimum(j, N_NAT_STEPS - 1))),
            pl.BlockSpec((D_MODEL, NAT_TN), lambda i, j: (0, jnp.minimum(j, N_NAT_STEPS - 1))),
            pl.BlockSpec((1, D_MODEL, DIL_TN), lambda i, j: (dil_step(j), 0, 0)),
        ],
        out_specs=[
            pl.BlockSpec((1, SEQ, NAT_TN), lambda i, j: (i, 0, jnp.minimum(j, N_NAT_STEPS - 1))),
            pl.BlockSpec((1, 1, SEQ, DIL_TN),
                         lambda i, j: (i, dil_step(j) // N_DIL_STEPS, 0, dil_step(j) % N_DIL_STEPS)),
        ],
        out_shape=[jax.ShapeDtypeStruct((b, SEQ, NAT_COLS), BF16),
                   jax.ShapeDtypeStruct((b, 2, SEQ, DIL_COLS), BF16)],
        scratch_shapes=[
            pltpu.VMEM((SEQ, D_MODEL), BF16),
            pltpu.VMEM((D_MODEL // LANES, SEQ, LANES), BF16),
            pltpu.VMEM((D_MODEL // LANES, SEQ, LANES), BF16),
            pltpu.VMEM((SEQ, LANES), F32),
            pltpu.VMEM((SEQ, LANES), F32),
            pltpu.VMEM((SEQ, LANES), F32),
        ],
        compiler_params=pltpu.CompilerParams(
            dimension_semantics=("arbitrary", "arbitrary"), vmem_limit_bytes=PROJ_VMEM_LIMIT),
        name="fused_proj",
    )(x, x, gain.reshape(1, D_MODEL), gain.reshape(1, D_MODEL), w_nat, w_dil)


def _pair_attention(q, k, vext, bias):
    rows = q.shape[0]
    head0 = lax.broadcasted_iota(jnp.int32, (rows, LANES), 1) < HEAD_DIM
    zero = jnp.zeros_like(q)
    q2 = jnp.concatenate([jnp.where(head0, q, zero), jnp.where(head0, zero, q)], axis=0)
    s = _dot_nt(q2, k) + bias
    m = jnp.max(s, axis=-1, keepdims=True)
    p = jnp.exp(s - m).astype(BF16)
    oe = jnp.dot(p, vext, preferred_element_type=F32)
    o = jnp.where(head0, oe[:rows, :LANES], oe[rows:, :LANES])
    l = jnp.where(head0, oe[:rows, LANES:], oe[rows:, LANES:])
    m = jnp.where(head0, m[:rows], m[rows:])
    return o, l, m


def _dilated_kernel(coef_ref, q0_ref, q1_ref, q2_ref, ag_ref, out_ref,
                    vext_ref, bias_ref, o_ref, l_ref, m_ref, stage_ref):
    hp = pl.program_id(0)
    scale = HEAD_DIM ** -0.5
    tk = 2 * QT
    n_tiles = SEQ // QT
    quarter = SEQ // STRIDE_STEP
    qkv_refs = (q0_ref, q1_ref, q2_ref)
    g_staged = [d for _, d in A_GROUPS].index(STRIDE_STEP ** 2)

    @pl.when(pl.program_id(1) == 0)
    def _():
        row = lax.broadcasted_iota(jnp.int32, (QT, tk), 0)
        col = lax.broadcasted_iota(jnp.int32, (QT, tk), 1)
        for g, (_, dil) in enumerate(A_GROUPS):
            vext_ref[g, :, LANES:] = jnp.ones((SEQ, LANES), BF16)
            if SEQ // dil > QT:
                variants = ((0, 0, tk), (-A_REACH, 0, tk), (-2 * A_REACH, 0, tk))
            else:
                variants = ((0, 0, QT), (-QT, QT, tk))
            for v, (off, lo, hi) in enumerate(variants):
                rel = jnp.abs(col + off - row)
                relf = rel.astype(F32)
                masked = (rel > A_REACH) | (col < lo) | (col >= hi)
                for hh in range(2):
                    c = coef_ref[g * A_HEADS + 2 * hp + hh]
                    bias_ref[g, v, hh * QT:(hh + 1) * QT, :] = jnp.where(masked, NEG_INF, -c * relf)

    for g, qkv_ref in enumerate(qkv_refs):
        vext_ref[g, :, :LANES] = qkv_ref[:, 2 * LANES:3 * LANES]

    def group_tile(u, g):
        dil = A_GROUPS[g][1]
        qkv_ref = qkv_refs[g]
        sub = SEQ // dil
        nt = sub // QT
        if nt == 1:
            r, t, vidx = u, 0, u % 2
            ks = (u % 2) * -QT
        else:
            r, t = u // nt, u % nt
            ks = jnp.clip(t * QT - A_REACH, 0, sub - tk)
            vidx = jnp.where(t == 0, 0, jnp.where(t == nt - 1, 2, 1))
        qrow = pl.multiple_of(r * sub + t * QT, QT)
        krow = pl.multiple_of(r * sub + ks, A_REACH)
        q = qkv_ref[pl.ds(qrow, QT), 0:LANES] * jnp.asarray(scale, BF16)
        k = qkv_ref[pl.ds(krow, tk), LANES:2 * LANES]
        vext = vext_ref[g, pl.ds(krow, tk), :]
        o, l, m = _pair_attention(q, k, vext, bias_ref[g, vidx])
        if dil == 1:
            dst = pl.ds(pl.multiple_of(t * QT, QT), QT)
        elif dil == STRIDE_STEP:
            dst = pl.ds(r + dil * QT * t, QT, stride=dil)
        else:
            dst = pl.ds((r // STRIDE_STEP) * quarter + r % STRIDE_STEP, QT, stride=STRIDE_STEP)
        if dil == STRIDE_STEP ** 2:
            stage_ref[0, dst, :] = o
            stage_ref[1, dst, :] = l
            stage_ref[2, dst, :] = m
        else:
            o_ref[g, dst, :] = o
            l_ref[g, dst, :] = l
            m_ref[g, dst, :] = m

    def tiles(u, carry):
        for g in range(len(A_GROUPS)):
            group_tile(u, g)
        return carry

    lax.fori_loop(0, n_tiles, tiles, 0, unroll=A_UNROLL)

    def interleave(ra, carry):
        src = pl.ds(pl.multiple_of(ra * quarter, quarter), quarter)
        for k, dst_ref in enumerate((o_ref, l_ref, m_ref)):
            dst_ref[g_staged, pl.ds(ra, quarter, stride=STRIDE_STEP), :] = stage_ref[k, src, :]
        return carry

    lax.fori_loop(0, STRIDE_STEP, interleave, 0)

    def combine(u, carry):
        sl = pl.ds(pl.multiple_of(u * QT, QT), QT)
        ms = [m_ref[g, sl, :] for g in range(3)]
        mm = jnp.maximum(jnp.maximum(ms[0], ms[1]), ms[2])
        ws = [jnp.exp(m - mm) for m in ms]
        num = ws[0] * o_ref[0, sl, :] + ws[1] * o_ref[1, sl, :] + ws[2] * o_ref[2, sl, :]
        den = ws[0] * l_ref[0, sl, :] + ws[1] * l_ref[1, sl, :] + ws[2] * l_ref[2, sl, :]
        gate = ag_ref[0, sl, :].astype(F32)
        out_ref[0, sl, :] = _gated_ratio(num, den, gate).astype(BF16)
        return carry

    lax.fori_loop(0, SEQ // QT, combine, 0)


def _dilated_mixer(coef, proj_nat, proj_dil):
    b = proj_nat.shape[0]
    grid_spec = pltpu.PrefetchScalarGridSpec(
        num_scalar_prefetch=1,
        grid=(N_PAIRS, b),
        in_specs=[
            pl.BlockSpec((None, SEQ, 3 * LANES), lambda j, i, c: (i, 0, COL_A_QKV // (3 * LANES) + j)),
            pl.BlockSpec((None, None, SEQ, 3 * LANES), lambda j, i, c: (i, 0, 0, j)),
            pl.BlockSpec((None, None, SEQ, 3 * LANES), lambda j, i, c: (i, 1, 0, j)),
            pl.BlockSpec((1, SEQ, LANES), lambda j, i, c: (i, 0, COL_AG // LANES + j)),
        ],
        out_specs=pl.BlockSpec((1, SEQ, LANES), lambda j, i, c: (i, 0, j)),
        scratch_shapes=[
            pltpu.VMEM((3, SEQ, 2 * LANES), BF16),
            pltpu.VMEM((3, 3, 2 * QT, 2 * QT), F32),
            pltpu.VMEM((3, SEQ, LANES), F32),
            pltpu.VMEM((3, SEQ, LANES), F32),
            pltpu.VMEM((3, SEQ, LANES), F32),
            pltpu.VMEM((3, SEQ, LANES), F32),
        ],
    )
    return pl.pallas_call(
        _dilated_kernel,
        grid_spec=grid_spec,
        out_shape=jax.ShapeDtypeStruct((b, SEQ, N_PAIRS * LANES), BF16),
        compiler_params=pltpu.CompilerParams(
            dimension_semantics=("arbitrary", "arbitrary"), vmem_limit_bytes=VMEM_LIMIT),
        name="dilated_mixer",
    )(coef, proj_nat, proj_dil, proj_dil, proj_nat)


def _neighbour_kernel(qkv_ref, bg_ref, tab_ref, out_ref, vext_ref):
    vext_ref[:, LANES:] = jnp.ones((SEQ, LANES), BF16)
    vext_ref[:, :LANES] = qkv_ref[0, :, 2 * LANES:3 * LANES]
    scale = HEAD_DIM ** -0.5
    wr = min(NA_ROWS, ROWS)

    def tile(r, carry):
        r0 = jnp.clip(r - wr // 2, 0, ROWS - wr)
        krow = pl.multiple_of(r0 * GRID_W, GRID_W)
        sl = pl.ds(pl.multiple_of(r * GRID_W, GRID_W), GRID_W)
        q = qkv_ref[0, sl, 0:LANES] * jnp.asarray(scale, BF16)
        k = qkv_ref[0, pl.ds(krow, B_KEYS), LANES:2 * LANES]
        vext = vext_ref[pl.ds(krow, B_KEYS), :]
        o, l, _ = _pair_attention(q, k, vext, tab_ref[0, r - r0])
        gate = bg_ref[0, sl, :].astype(F32)
        out_ref[0, sl, :] = _gated_ratio(o, l, gate).astype(BF16)
        return carry

    lax.fori_loop(0, ROWS, tile, 0, unroll=B_UNROLL)


def _neighbour_table(rpb):
    wr = min(NA_ROWS, ROWS)
    pad = GRID_W - NA_COLS
    rpb_pad = jnp.pad(rpb, ((0, 0), (0, 0), (pad, pad)))
    toeplitz = jnp.stack([rpb_pad[:, :, GRID_W - 1 - c:2 * GRID_W - 1 - c] for c in range(GRID_W)], axis=2)
    variants = []
    for d in range(wr):
        blocks = [toeplitz[:, w - d + NA_ROWS - 1] for w in range(wr)]
        variants.append(jnp.stack(blocks, axis=2).reshape(A_HEADS, GRID_W, B_KEYS))
    tab = jnp.stack(variants, axis=1)
    c = np.arange(GRID_W)
    kc = np.arange(B_KEYS) % GRID_W
    sc = np.clip(c - NA_COLS // 2, 0, GRID_W - NA_COLS)
    col_ok = (kc[None, :] >= sc[:, None]) & (kc[None, :] < sc[:, None] + NA_COLS)
    tab = jnp.where(col_ok[None, None], tab, NEG_INF)
    tab = tab.reshape(N_PAIRS, 2, wr, GRID_W, B_KEYS).transpose(0, 2, 1, 3, 4)
    return tab.reshape(N_PAIRS, wr, 2 * GRID_W, B_KEYS)


def _neighbour_mixer(proj_nat, table):
    b = proj_nat.shape[0]
    return pl.pallas_call(
        _neighbour_kernel,
        grid=(N_PAIRS, b),
        in_specs=[
            pl.BlockSpec((1, SEQ, 3 * LANES), lambda j, i: (i, 0, COL_B_QKV // (3 * LANES) + j)),
            pl.BlockSpec((1, SEQ, LANES), lambda j, i: (i, 0, COL_BG // LANES + j)),
            pl.BlockSpec((1, min(NA_ROWS, ROWS), 2 * GRID_W, B_KEYS), lambda j, i: (j, 0, 0, 0)),
        ],
        out_specs=pl.BlockSpec((1, SEQ, LANES), lambda j, i: (i, 0, j)),
        out_shape=jax.ShapeDtypeStruct((b, SEQ, N_PAIRS * LANES), BF16),
        scratch_shapes=[pltpu.VMEM((SEQ, 2 * LANES), BF16)],
        compiler_params=pltpu.CompilerParams(
            dimension_semantics=("arbitrary", "arbitrary"), vmem_limit_bytes=VMEM_LIMIT),
        name="neighbour_mixer",
    )(proj_nat, proj_nat, table)


def _memory_kernel(q_ref, mg_ref, mk_ref, mv_ref, out_ref, vext_ref):
    vext_ref[:, LANES:] = jnp.ones((N_MEM, LANES), BF16)
    vext_ref[:, :LANES] = mv_ref[0]
    scale = M_HEAD_DIM ** -0.5

    def tile(t, carry):
        sl = pl.ds(pl.multiple_of(t * M_QT, M_QT), M_QT)
        s = _dot_nt(q_ref[0, sl, :], mk_ref[0]) * scale
        m = jnp.max(s, axis=-1, keepdims=True)
        p = jnp.exp(s - m).astype(BF16)
        oe = jnp.dot(p, vext_ref[...], preferred_element_type=F32)
        gate = mg_ref[0, sl, :].astype(F32)
        out_ref[0, sl, :] = _gated_ratio(oe[:, :LANES], oe[:, LANES:], gate).astype(BF16)
        return carry

    lax.fori_loop(0, SEQ // M_QT, tile, 0, unroll=M_UNROLL)


def _memory_mixer(proj_nat, mem_kv):
    b = proj_nat.shape[0]
    return pl.pallas_call(
        _memory_kernel,
        grid=(b, M_HEADS),
        in_specs=[
            pl.BlockSpec((1, SEQ, LANES), lambda i, j: (i, 0, COL_MQ // LANES + j)),
            pl.BlockSpec((1, SEQ, LANES), lambda i, j: (i, 0, COL_MG // LANES + j)),
            pl.BlockSpec((1, N_MEM, LANES), lambda i, j: (i, 0, j)),
            pl.BlockSpec((1, N_MEM, LANES), lambda i, j: (i, 0, M_HEADS + j)),
        ],
        out_specs=pl.BlockSpec((1, SEQ, LANES), lambda i, j: (i, 0, j)),
        out_shape=jax.ShapeDtypeStruct((b, SEQ, M_HEADS * LANES), BF16),
        scratch_shapes=[pltpu.VMEM((N_MEM, 2 * LANES), BF16)],
        compiler_params=pltpu.CompilerParams(
            dimension_semantics=("arbitrary", "arbitrary"), vmem_limit_bytes=VMEM_LIMIT),
        name="memory_mixer",
    )(proj_nat, proj_nat, mem_kv, mem_kv)


def _merge_out_kernel(ga_ref, gb_ref, gm_ref, s0_ref, s1_ref, s2_ref, x_ref,
                      wa_ref, wb_ref, wm_ref, wo_ref, g_ref, out_ref):
    merged = None
    for br_ref, w_ref, s_ref in ((ga_ref, wa_ref, s0_ref), (gb_ref, wb_ref, s1_ref),
                                 (gm_ref, wm_ref, s2_ref)):
        branch = jnp.dot(br_ref[0], w_ref[...], preferred_element_type=F32)
        term = jax.nn.sigmoid(s_ref[0].astype(F32)) * branch
        merged = term if merged is None else merged + term
    y = x_ref[0] + jnp.dot(merged.astype(BF16), wo_ref[...], preferred_element_type=F32)
    ms = jnp.mean(y * y, axis=-1, keepdims=True)
    out_ref[0] = (y * lax.rsqrt(ms + RMS_EPS)) * g_ref[...]


def _merge_out(ga, gb, gm, proj_nat, x, wa, wb, wm, wo, gain, *, tm=512):
    b = x.shape[0]
    width = ga.shape[-1]
    br_spec = pl.BlockSpec((1, tm, width), lambda i, j: (i, j, 0))
    w_spec = pl.BlockSpec((width, D_MODEL), lambda i, j: (0, 0))
    gate_specs = [
        pl.BlockSpec((1, tm, D_MODEL), functools.partial(lambda i, j, n: (i, j, COL_MERGE // D_MODEL + n), n=n))
        for n in range(3)
    ]
    return pl.pallas_call(
        _merge_out_kernel,
        grid=(b, SEQ // tm),
        in_specs=[br_spec, br_spec, br_spec, *gate_specs,
                  pl.BlockSpec((1, tm, D_MODEL), lambda i, j: (i, j, 0)),
                  w_spec, w_spec, w_spec,
                  pl.BlockSpec((D_MODEL, D_MODEL), lambda i, j: (0, 0)),
                  pl.BlockSpec((1, D_MODEL), lambda i, j: (0, 0))],
        out_specs=pl.BlockSpec((1, tm, D_MODEL), lambda i, j: (i, j, 0)),
        out_shape=jax.ShapeDtypeStruct((b, SEQ, D_MODEL), F32),
        compiler_params=pltpu.CompilerParams(
            dimension_semantics=("arbitrary", "arbitrary"), vmem_limit_bytes=VMEM_LIMIT),
        name="merge_out",
    )(ga, gb, gm, proj_nat, proj_nat, proj_nat, x, wa, wb, wm, wo, gain.reshape(1, D_MODEL))


def _prepare_weights(w_in):
    a_w = 3 * A_HEADS * HEAD_DIM
    b_w = A_HEADS * HEAD_DIM
    aq, ak, av = (w_in[:, i * a_w:(i + 1) * a_w].reshape(D_MODEL, 3, N_PAIRS, LANES) for i in range(3))
    a_qkv = jnp.stack([aq, ak, av], axis=3)
    off = 3 * a_w
    ag = w_in[:, off:off + b_w]
    off += b_w
    bq, bk, bv = (w_in[:, off + i * b_w:off + (i + 1) * b_w].reshape(D_MODEL, N_PAIRS, LANES) for i in range(3))
    b_qkv = jnp.stack([bq, bk, bv], axis=2).reshape(D_MODEL, 3 * b_w)
    off += 3 * b_w
    bg = w_in[:, off:off + b_w]
    rest = w_in[:, off + b_w:]
    w_nat = jnp.concatenate([a_qkv[:, 0].reshape(D_MODEL, a_w), b_qkv, ag, bg, rest], axis=1)
    w_dil = a_qkv[:, 1:].reshape(D_MODEL, 2 * N_DIL_STEPS, DIL_TN).transpose(1, 0, 2)
    return w_nat.astype(BF16), w_dil.astype(BF16)


def _alibi_coefficients():
    n_heads = len(A_GROUPS) * A_HEADS
    slopes = jnp.exp2(-8.0 * jnp.arange(1, n_heads + 1, dtype=F32) / n_heads)
    dils = np.repeat(np.array([d for _, d in A_GROUPS], np.float32), A_HEADS)
    return slopes * dils


def _trunk(x, mem, weights):
    (g_norm, g_mem, w_nat, w_dil, w_mem, table, coef, wa, wb, wm, wo, g_final) = weights
    proj_nat, proj_dil = _fused_proj(x, g_norm, w_nat, w_dil)
    mem_kv = _norm_proj(mem, g_mem, w_mem, dil=1, tn=1024)
    ga = _dilated_mixer(coef, proj_nat, proj_dil)
    gb = _neighbour_mixer(proj_nat, table)
    gm = _memory_mixer(proj_nat, mem_kv)
    return _merge_out(ga, gb, gm, proj_nat, x, wa, wb, wm, wo, g_final)


def kernel(x_prompt, x_sample, mem_prompt, mem_sample, norm_gain, mem_norm_gain, w_in, w_mem_kv,
           rpb, w_proj_a, w_proj_b, w_proj_m, w_out, final_norm_gain):
    assert norm_gain.shape[0] == 1, "single-layer trunk"
    w_nat, w_dil = _prepare_weights(w_in[0])
    weights = (norm_gain[0], mem_norm_gain[0], w_nat, w_dil, w_mem_kv[0].astype(BF16),
               _neighbour_table(rpb[0]), _alibi_coefficients(),
               w_proj_a[0].astype(BF16), w_proj_b[0].astype(BF16), w_proj_m[0].astype(BF16),
               w_out[0].astype(BF16), final_norm_gain)
    return (_trunk(x_prompt, mem_prompt, weights), _trunk(x_sample, mem_sample, weights))
```

```python
import functools

import numpy as np
import jax
import jax.numpy as jnp
from jax import lax
from jax.experimental import pallas as pl
from jax.experimental.pallas import tpu as pltpu

F32 = jnp.float32
BF16 = jnp.bfloat16

D_MODEL = 1024
SEQ = 2048
N_MEM = 256
GRID_W = 64
ROWS = SEQ // GRID_W
HEAD_DIM = 64
A_GROUPS = ((128, 1), (512, 4), (2048, 16))
A_HEADS = 8
NA_ROWS = 8
NA_COLS = 16
M_HEADS = 4
M_HEAD_DIM = 128
RMS_EPS = 1e-6
NEG_INF = -1e30
LOG2_E = 1.4426950408889634

LANES = 128
N_PAIRS = A_HEADS // 2
QT = 128
STRIDE_STEP = 4
A_REACH = 64
B_KEYS = min(NA_ROWS, ROWS) * GRID_W
M_QT = 256
A_UNROLL = 8
B_UNROLL = 32
M_UNROLL = 8

NAT_COLS = 8192
COL_A_QKV = 0
COL_B_QKV = 1536
COL_AG = 3072
COL_BG = 3584
COL_MQ = 4096
COL_MG = 4608
COL_MERGE = 5120
DIL_COLS = 1536

VMEM_LIMIT = 52 * 1024 * 1024
PROJ_VMEM_LIMIT = 56 * 1024 * 1024


def _gated_ratio(num, den, gate):
    return (num * gate) / (den * (1.0 + jnp.exp(-gate)))


def _dot_nt(a, b):
    return lax.dot_general(a, b, (((1,), (1,)), ((), ())), preferred_element_type=F32)


def _norm_proj_kernel(x_ref, g_ref, w_ref, o_ref, h_ref, *, dil, seq):
    sub = seq // dil
    rows = min(sub, 256)

    @pl.when(pl.program_id(1) == 0)
    def _():
        g = g_ref[...]
        for r in range(dil):
            for c in range(sub // rows):
                xs = x_ref[0, c * rows:(c + 1) * rows, r * D_MODEL:(r + 1) * D_MODEL]
                ms = jnp.mean(xs * xs, axis=-1, keepdims=True)
                h = (xs * lax.rsqrt(ms + RMS_EPS)) * g
                h_ref[r * sub + c * rows:r * sub + (c + 1) * rows, :] = h.astype(BF16)

    mt = min(seq, 512)
    for c in range(seq // mt):
        acc = jnp.dot(h_ref[c * mt:(c + 1) * mt, :], w_ref[...], preferred_element_type=F32)
        o_ref[0, c * mt:(c + 1) * mt, :] = acc.astype(BF16)


def _norm_proj(x, gain, w, *, dil, tn):
    b, seq, _ = x.shape
    n = w.shape[1]
    sub = seq // dil
    xv = x.reshape(b, sub, dil * D_MODEL)
    return pl.pallas_call(
        functools.partial(_norm_proj_kernel, dil=dil, seq=seq),
        grid=(b, n // tn),
        in_specs=[
            pl.BlockSpec((1, sub, dil * D_MODEL), lambda i, j: (i, 0, 0)),
            pl.BlockSpec((1, D_MODEL), lambda i, j: (0, 0)),
            pl.BlockSpec((D_MODEL, tn), lambda i, j: (0, j)),
        ],
        out_specs=pl.BlockSpec((1, seq, tn), lambda i, j: (i, 0, j)),
        out_shape=jax.ShapeDtypeStruct((b, seq, n), BF16),
        scratch_shapes=[pltpu.VMEM((seq, D_MODEL), BF16)],
        compiler_params=pltpu.CompilerParams(
            dimension_semantics=("arbitrary", "arbitrary"), vmem_limit_bytes=VMEM_LIMIT),
        name=f"norm_proj_d{dil}",
    )(xv, gain.reshape(1, D_MODEL), w)


NAT_TN = 1024
N_NAT_STEPS = NAT_COLS // NAT_TN
DIL_TN = 768
N_DIL_STEPS = DIL_COLS // DIL_TN
PROJ_MT = 512


def _fused_proj_kernel(x_ref, xs_ref, g_ref, gs_ref, wn_ref, wd_ref, on_ref, od_ref,
                       hn_ref, h4_ref, h16_ref, inv_ref, sa_ref, sb_ref):
    assert N_NAT_STEPS == D_MODEL // LANES, "one lane slab is regrouped per natural-order column step"
    j = pl.program_id(1)
    quarter = SEQ // STRIDE_STEP

    @pl.when(j == 0)
    def _():
        g = g_ref[...]
        rows = 256
        for c in range(SEQ // rows):
            sl = slice(c * rows, (c + 1) * rows)
            xs = x_ref[0, sl, :]
            inv = lax.rsqrt(jnp.mean(xs * xs, axis=-1, keepdims=True) + RMS_EPS)
            inv_ref[sl, :] = jnp.broadcast_to(inv, (rows, LANES))
            hn_ref[sl, :] = ((xs * inv) * g).astype(BF16)

    @pl.when(j < N_NAT_STEPS)
    def _():
        assert SEQ // PROJ_MT == STRIDE_STEP
        sa_ref[...] = (xs_ref[0] * inv_ref[...]) * gs_ref[...]
        for c in range(SEQ // PROJ_MT):
            sl = slice(c * PROJ_MT, (c + 1) * PROJ_MT)
            on_ref[0, sl, :] = jnp.dot(hn_ref[sl, :], wn_ref[...], preferred_element_type=F32).astype(BF16)
            part = sa_ref[pl.ds(c, quarter, stride=STRIDE_STEP), :]
            sb_ref[c * quarter:(c + 1) * quarter, :] = part
            h4_ref[j, c * quarter:(c + 1) * quarter, :] = part.astype(BF16)
            for rb in range(STRIDE_STEP):
                slot = c * STRIDE_STEP + rb
                h16_ref[j, slot * QT:(slot + 1) * QT, :] = (
                    sb_ref[pl.ds(c * quarter + rb, QT, stride=STRIDE_STEP), :].astype(BF16))

    for group, h_ref in enumerate((h4_ref, h16_ref)):
        first = N_NAT_STEPS + group * N_DIL_STEPS

        @pl.when((j >= first) & (j < first + N_DIL_STEPS))
        def _(h_ref=h_ref):
            for c in range(SEQ // PROJ_MT):
                sl = slice(c * PROJ_MT, (c + 1) * PROJ_MT)
                h = jnp.concatenate([h_ref[k, sl, :] for k in range(N_NAT_STEPS)], axis=1)
                od_ref[0, 0, sl, :] = jnp.dot(h, wd_ref[0], preferred_element_type=F32).astype(BF16)


def _fused_proj(x, gain, w_nat, w_dil):
    b = x.shape[0]

    def dil_step(j):
        return jnp.clip(j - N_NAT_STEPS, 0, 2 * N_DIL_STEPS - 1)

    return pl.pallas_call(
        _fused_proj_kernel,
        grid=(b, N_NAT_STEPS + 2 * N_DIL_STEPS),
        in_specs=[
            pl.BlockSpec((1, SEQ, D_MODEL), lambda i, j: (i, 0, 0)),
            pl.BlockSpec((1, SEQ, LANES), lambda i, j: (i, 0, jnp.minimum(j, N_NAT_STEPS - 1))),
            pl.BlockSpec((1, D_MODEL), lambda i, j: (0, 0)),
            pl.BlockSpec((1, LANES), lambda i, j: (0, jnp.minimum(j, N_NAT_STEPS - 1))),
            pl.BlockSpec((D_MODEL, NAT_TN), lambda i, j: (0, jnp.minimum(j, N_NAT_STEPS - 1))),
            pl.BlockSpec((1, D_MODEL, DIL_TN), lambda i, j: (dil_step(j), 0, 0)),
        ],
        out_specs=[
            pl.BlockSpec((1, SEQ, NAT_TN), lambda i, j: (i, 0, jnp.minimum(j, N_NAT_STEPS - 1))),
            pl.BlockSpec((1, 1, SEQ, DIL_TN),
                         lambda i, j: (i, dil_step(j) // N_DIL_STEPS, 0, dil_step(j) % N_DIL_STEPS)),
        ],
        out_shape=[jax.ShapeDtypeStruct((b, SEQ, NAT_COLS), BF16),
                   jax.ShapeDtypeStruct((b, 2, SEQ, DIL_COLS), BF16)],
        scratch_shapes=[
            pltpu.VMEM((SEQ, D_MODEL), BF16),
            pltpu.VMEM((D_MODEL // LANES, SEQ, LANES), BF16),
            pltpu.VMEM((D_MODEL // LANES, SEQ, LANES), BF16),
            pltpu.VMEM((SEQ, LANES), F32),
            pltpu.VMEM((SEQ, LANES), F32),
            pltpu.VMEM((SEQ, LANES), F32),
        ],
        compiler_params=pltpu.CompilerParams(
            dimension_semantics=("arbitrary", "arbitrary"), vmem_limit_bytes=PROJ_VMEM_LIMIT),
        name="fused_proj",
    )(x, x, gain.reshape(1, D_MODEL), gain.reshape(1, D_MODEL), w_nat, w_dil)


def _pair_attention(q, k, vext, bias):
    rows = q.shape[0]
    head0 = lax.broadcasted_iota(jnp.int32, (rows, LANES), 1) < HEAD_DIM
    zero = jnp.zeros_like(q)
    q2 = jnp.concatenate([jnp.where(head0, q, zero), jnp.where(head0, zero, q)], axis=0)
    s = _dot_nt(q2, k) + bias
    m = jnp.max(s, axis=-1, keepdims=True)
    p = jnp.exp(s - m).astype(BF16)
    oe = jnp.dot(p, vext, preferred_element_type=F32)
    o = jnp.where(head0, oe[:rows, :LANES], oe[rows:, :LANES])
    l = jnp.where(head0, oe[:rows, LANES:], oe[rows:, LANES:])
    m = jnp.where(head0, m[:rows], m[rows:])
    return o, l, m


def _dilated_kernel(coef_ref, q0_ref, q1_ref, q2_ref, ag_ref, out_ref,
                    vext_ref, bias_ref, o_ref, l_ref, m_ref, stage_ref):
    hp = pl.program_id(0)
    scale = HEAD_DIM ** -0.5
    tk = 2 * QT
    n_tiles = SEQ // QT
    quarter = SEQ // STRIDE_STEP
    qkv_refs = (q0_ref, q1_ref, q2_ref)
    g_staged = [d for _, d in A_GROUPS].index(STRIDE_STEP ** 2)

    @pl.when(pl.program_id(1) == 0)
    def _():
        row = lax.broadcasted_iota(jnp.int32, (QT, tk), 0)
        col = lax.broadcasted_iota(jnp.int32, (QT, tk), 1)
        for g, (_, dil) in enumerate(A_GROUPS):
            vext_ref[g, :, LANES:] = jnp.ones((SEQ, LANES), BF16)
            if SEQ // dil > QT:
                variants = ((0, 0, tk), (-A_REACH, 0, tk), (-2 * A_REACH, 0, tk))
            else:
                variants = ((0, 0, QT), (-QT, QT, tk))
            for v, (off, lo, hi) in enumerate(variants):
                rel = jnp.abs(col + off - row)
                relf = rel.astype(F32)
                masked = (rel > A_REACH) | (col < lo) | (col >= hi)
                for hh in range(2):
                    c = coef_ref[g * A_HEADS + 2 * hp + hh]
                    bias_ref[g, v, hh * QT:(hh + 1) * QT, :] = jnp.where(masked, NEG_INF, -c * relf)

    for g, qkv_ref in enumerate(qkv_refs):
        vext_ref[g, :, :LANES] = qkv_ref[:, 2 * LANES:3 * LANES]

    def group_tile(u, g):
        dil = A_GROUPS[g][1]
        qkv_ref = qkv_refs[g]
        sub = SEQ // dil
        nt = sub // QT
        if nt == 1:
            r, t, vidx = u, 0, u % 2
            ks = (u % 2) * -QT
        else:
            r, t = u // nt, u % nt
            ks = jnp.clip(t * QT - A_REACH, 0, sub - tk)
            vidx = jnp.where(t == 0, 0, jnp.where(t == nt - 1, 2, 1))
        qrow = pl.multiple_of(r * sub + t * QT, QT)
        krow = pl.multiple_of(r * sub + ks, A_REACH)
        q = qkv_ref[pl.ds(qrow, QT), 0:LANES] * jnp.asarray(scale, BF16)
        k = qkv_ref[pl.ds(krow, tk), LANES:2 * LANES]
        vext = vext_ref[g, pl.ds(krow, tk), :]
        o, l, m = _pair_attention(q, k, vext, bias_ref[g, vidx])
        if dil == 1:
            dst = pl.ds(pl.multiple_of(t * QT, QT), QT)
        elif dil == STRIDE_STEP:
            dst = pl.ds(r + dil * QT * t, QT, stride=dil)
        else:
            dst = pl.ds((r // STRIDE_STEP) * quarter + r % STRIDE_STEP, QT, stride=STRIDE_STEP)
        if dil == STRIDE_STEP ** 2:
            stage_ref[0, dst, :] = o
            stage_ref[1, dst, :] = l
            stage_ref[2, dst, :] = m
        else:
            o_ref[g, dst, :] = o
            l_ref[g, dst, :] = l
            m_ref[g, dst, :] = m

    def tiles(u, carry):
        for g in range(len(A_GROUPS)):
            group_tile(u, g)
        return carry

    lax.fori_loop(0, n_tiles, tiles, 0, unroll=A_UNROLL)

    def interleave(ra, carry):
        src = pl.ds(pl.multiple_of(ra * quarter, quarter), quarter)
        for k, dst_ref in enumerate((o_ref, l_ref, m_ref)):
            dst_ref[g_staged, pl.ds(ra, quarter, stride=STRIDE_STEP), :] = stage_ref[k, src, :]
        return carry

    lax.fori_loop(0, STRIDE_STEP, interleave, 0)

    def combine(u, carry):
        sl = pl.ds(pl.multiple_of(u * QT, QT), QT)
        ms = [m_ref[g, sl, :] for g in range(3)]
        mm = jnp.maximum(jnp.maximum(ms[0], ms[1]), ms[2])
        ws = [jnp.exp(m - mm) for m in ms]
        num = ws[0] * o_ref[0, sl, :] + ws[1] * o_ref[1, sl, :] + ws[2] * o_ref[2, sl, :]
        den = ws[0] * l_ref[0, sl, :] + ws[1] * l_ref[1, sl, :] + ws[2] * l_ref[2, sl, :]
        gate = ag_ref[0, sl, :].astype(F32)
        out_ref[0, sl, :] = _gated_ratio(num, den, gate).astype(BF16)
        return carry

    lax.fori_loop(0, SEQ // QT, combine, 0, unroll=4)


def _dilated_mixer(coef, proj_nat, proj_dil):
    b = proj_nat.shape[0]
    grid_spec = pltpu.PrefetchScalarGridSpec(
        num_scalar_prefetch=1,
        grid=(N_PAIRS, b),
        in_specs=[
            pl.BlockSpec((None, SEQ, 3 * LANES), lambda j, i, c: (i, 0, COL_A_QKV // (3 * LANES) + j)),
            pl.BlockSpec((None, None, SEQ, 3 * LANES), lambda j, i, c: (i, 0, 0, j)),
            pl.BlockSpec((None, None, SEQ, 3 * LANES), lambda j, i, c: (i, 1, 0, j)),
            pl.BlockSpec((1, SEQ, LANES), lambda j, i, c: (i, 0, COL_AG // LANES + j)),
        ],
        out_specs=pl.BlockSpec((1, SEQ, LANES), lambda j, i, c: (i, 0, j)),
        scratch_shapes=[
            pltpu.VMEM((3, SEQ, 2 * LANES), BF16),
            pltpu.VMEM((3, 3, 2 * QT, 2 * QT), F32),
            pltpu.VMEM((3, SEQ, LANES), F32),
            pltpu.VMEM((3, SEQ, LANES), F32),
            pltpu.VMEM((3, SEQ, LANES), F32),
            pltpu.VMEM((3, SEQ, LANES), F32),
        ],
    )
    return pl.pallas_call(
        _dilated_kernel,
        grid_spec=grid_spec,
        out_shape=jax.ShapeDtypeStruct((b, SEQ, N_PAIRS * LANES), BF16),
        compiler_params=pltpu.CompilerParams(
            dimension_semantics=("arbitrary", "arbitrary"), vmem_limit_bytes=VMEM_LIMIT),
        name="dilated_mixer",
    )(coef, proj_nat, proj_dil, proj_dil, proj_nat)


def _neighbour_kernel(qkv_ref, bg_ref, tab_ref, out_ref, vext_ref):
    vext_ref[:, LANES:] = jnp.ones((SEQ, LANES), BF16)
    vext_ref[:, :LANES] = qkv_ref[0, :, 2 * LANES:3 * LANES]
    scale = HEAD_DIM ** -0.5
    wr = min(NA_ROWS, ROWS)

    def tile(r, carry):
        r0 = jnp.clip(r - wr // 2, 0, ROWS - wr)
        krow = pl.multiple_of(r0 * GRID_W, GRID_W)
        sl = pl.ds(pl.multiple_of(r * GRID_W, GRID_W), GRID_W)
        q = qkv_ref[0, sl, 0:LANES] * jnp.asarray(scale, BF16)
        k = qkv_ref[0, pl.ds(krow, B_KEYS), LANES:2 * LANES]
        vext = vext_ref[pl.ds(krow, B_KEYS), :]
        o, l, _ = _pair_attention(q, k, vext, tab_ref[0, r - r0])
        gate = bg_ref[0, sl, :].astype(F32)
        out_ref[0, sl, :] = _gated_ratio(o, l, gate).astype(BF16)
        return carry

    lax.fori_loop(0, ROWS, tile, 0, unroll=B_UNROLL)


def _neighbour_table(rpb):
    wr = min(NA_ROWS, ROWS)
    pad = GRID_W - NA_COLS
    rpb_pad = jnp.pad(rpb, ((0, 0), (0, 0), (pad, pad)))
    toeplitz = jnp.stack([rpb_pad[:, :, GRID_W - 1 - c:2 * GRID_W - 1 - c] for c in range(GRID_W)], axis=2)
    variants = []
    for d in range(wr):
        blocks = [toeplitz[:, w - d + NA_ROWS - 1] for w in range(wr)]
        variants.append(jnp.stack(blocks, axis=2).reshape(A_HEADS, GRID_W, B_KEYS))
    tab = jnp.stack(variants, axis=1)
    c = np.arange(GRID_W)
    kc = np.arange(B_KEYS) % GRID_W
    sc = np.clip(c - NA_COLS // 2, 0, GRID_W - NA_COLS)
    col_ok = (kc[None, :] >= sc[:, None]) & (kc[None, :] < sc[:, None] + NA_COLS)
    tab = jnp.where(col_ok[None, None], tab, NEG_INF)
    tab = tab.reshape(N_PAIRS, 2, wr, GRID_W, B_KEYS).transpose(0, 2, 1, 3, 4)
    return tab.reshape(N_PAIRS, wr, 2 * GRID_W, B_KEYS)


def _neighbour_mixer(proj_nat, table):
    b = proj_nat.shape[0]
    return pl.pallas_call(
        _neighbour_kernel,
        grid=(N_PAIRS, b),
        in_specs=[
            pl.BlockSpec((1, SEQ, 3 * LANES), lambda j, i: (i, 0, COL_B_QKV // (3 * LANES) + j)),
            pl.BlockSpec((1, SEQ, LANES), lambda j, i: (i, 0, COL_BG // LANES + j)),
            pl.BlockSpec((1, min(NA_ROWS, ROWS), 2 * GRID_W, B_KEYS), lambda j, i: (j, 0, 0, 0)),
        ],
        out_specs=pl.BlockSpec((1, SEQ, LANES), lambda j, i: (i, 0, j)),
        out_shape=jax.ShapeDtypeStruct((b, SEQ, N_PAIRS * LANES), BF16),
        scratch_shapes=[pltpu.VMEM((SEQ, 2 * LANES), BF16)],
        compiler_params=pltpu.CompilerParams(
            dimension_semantics=("arbitrary", "arbitrary"), vmem_limit_bytes=VMEM_LIMIT),
        name="neighbour_mixer",
    )(proj_nat, proj_nat, table)


def _memory_kernel(q_ref, mg_ref, mkv_ref, out_ref, vext_ref):
    width = M_HEADS * M_HEAD_DIM
    scale = M_HEAD_DIM ** -0.5
    for h in range(M_HEADS):
        vext_ref[h, :, LANES:] = jnp.ones((N_MEM, LANES), BF16)
        vext_ref[h, :, :LANES] = mkv_ref[0, :, width + h * LANES:width + (h + 1) * LANES]

    def tile(t, carry):
        sl = pl.ds(pl.multiple_of(t * M_QT, M_QT), M_QT)
        for h in range(M_HEADS):
            cols = slice(h * LANES, (h + 1) * LANES)
            s = _dot_nt(q_ref[0, sl, cols], mkv_ref[0, :, cols])
            m = jnp.max(s, axis=-1, keepdims=True)
            p = jnp.exp2((s - m) * (scale * LOG2_E)).astype(BF16)
            oe = jnp.dot(p, vext_ref[h], preferred_element_type=F32)
            gate = mg_ref[0, sl, cols].astype(F32)
            out_ref[0, sl, cols] = _gated_ratio(oe[:, :LANES], oe[:, LANES:], gate).astype(BF16)
        return carry

    lax.fori_loop(0, SEQ // M_QT, tile, 0, unroll=M_UNROLL)


def _memory_mixer(proj_nat, mem_kv):
    b = proj_nat.shape[0]
    width = M_HEADS * M_HEAD_DIM
    assert M_HEAD_DIM == LANES and COL_MQ % width == 0 and COL_MG % width == 0
    return pl.pallas_call(
        _memory_kernel,
        grid=(b,),
        in_specs=[
            pl.BlockSpec((1, SEQ, width), lambda i: (i, 0, COL_MQ // width)),
            pl.BlockSpec((1, SEQ, width), lambda i: (i, 0, COL_MG // width)),
            pl.BlockSpec((1, N_MEM, 2 * width), lambda i: (i, 0, 0)),
        ],
        out_specs=pl.BlockSpec((1, SEQ, width), lambda i: (i, 0, 0)),
        out_shape=jax.ShapeDtypeStruct((b, SEQ, width), BF16),
        scratch_shapes=[pltpu.VMEM((M_HEADS, N_MEM, 2 * LANES), BF16)],
        compiler_params=pltpu.CompilerParams(
            dimension_semantics=("arbitrary",), vmem_limit_bytes=VMEM_LIMIT),
        name="memory_mixer",
    )(proj_nat, proj_nat, mem_kv)


def _merge_out_kernel(ga_ref, gb_ref, gm_ref, s0_ref, s1_ref, s2_ref, x_ref,
                      wa_ref, wb_ref, wm_ref, wo_ref, g_ref, out_ref):
    merged = None
    for br_ref, w_ref, s_ref in ((ga_ref, wa_ref, s0_ref), (gb_ref, wb_ref, s1_ref),
                                 (gm_ref, wm_ref, s2_ref)):
        branch = jnp.dot(br_ref[0], w_ref[...], preferred_element_type=F32)
        term = jax.nn.sigmoid(s_ref[0].astype(F32)) * branch
        merged = term if merged is None else merged + term
    y = x_ref[0] + jnp.dot(merged.astype(BF16), wo_ref[...], preferred_element_type=F32)
    ms = jnp.mean(y * y, axis=-1, keepdims=True)
    out_ref[0] = (y * lax.rsqrt(ms + RMS_EPS)) * g_ref[...]


def _merge_out(ga, gb, gm, proj_nat, x, wa, wb, wm, wo, gain, *, tm=512):
    b = x.shape[0]
    width = ga.shape[-1]
    br_spec = pl.BlockSpec((1, tm, width), lambda i, j: (i, j, 0))
    w_spec = pl.BlockSpec((width, D_MODEL), lambda i, j: (0, 0))
    gate_specs = [
        pl.BlockSpec((1, tm, D_MODEL), functools.partial(lambda i, j, n: (i, j, COL_MERGE // D_MODEL + n), n=n))
        for n in range(3)
    ]
    return pl.pallas_call(
        _merge_out_kernel,
        grid=(b, SEQ // tm),
        in_specs=[br_spec, br_spec, br_spec, *gate_specs,
                  pl.BlockSpec((1, tm, D_MODEL), lambda i, j: (i, j, 0)),
                  w_spec, w_spec, w_spec,
                  pl.BlockSpec((D_MODEL, D_MODEL), lambda i, j: (0, 0)),
                  pl.BlockSpec((1, D_MODEL), lambda i, j: (0, 0))],
        out_specs=pl.BlockSpec((1, tm, D_MODEL), lambda i, j: (i, j, 0)),
        out_shape=jax.ShapeDtypeStruct((b, SEQ, D_MODEL), F32),
        compiler_params=pltpu.CompilerParams(
            dimension_semantics=("arbitrary", "arbitrary"), vmem_limit_bytes=VMEM_LIMIT),
        name="merge_out",
    )(ga, gb, gm, proj_nat, proj_nat, proj_nat, x, wa, wb, wm, wo, gain.reshape(1, D_MODEL))


def _prepare_weights(w_in):
    a_w = 3 * A_HEADS * HEAD_DIM
    b_w = A_HEADS * HEAD_DIM
    aq, ak, av = (w_in[:, i * a_w:(i + 1) * a_w].reshape(D_MODEL, 3, N_PAIRS, LANES) for i in range(3))
    a_qkv = jnp.stack([aq, ak, av], axis=3)
    off = 3 * a_w
    ag = w_in[:, off:off + b_w]
    off += b_w
    bq, bk, bv = (w_in[:, off + i * b_w:off + (i + 1) * b_w].reshape(D_MODEL, N_PAIRS, LANES) for i in range(3))
    b_qkv = jnp.stack([bq, bk, bv], axis=2).reshape(D_MODEL, 3 * b_w)
    off += 3 * b_w
    bg = w_in[:, off:off + b_w]
    rest = w_in[:, off + b_w:]
    w_nat = jnp.concatenate([a_qkv[:, 0].reshape(D_MODEL, a_w), b_qkv, ag, bg, rest], axis=1)
    w_dil = a_qkv[:, 1:].reshape(D_MODEL, 2 * N_DIL_STEPS, DIL_TN).transpose(1, 0, 2)
    return w_nat.astype(BF16), w_dil.astype(BF16)


def _alibi_coefficients():
    n_heads = len(A_GROUPS) * A_HEADS
    slopes = jnp.exp2(-8.0 * jnp.arange(1, n_heads + 1, dtype=F32) / n_heads)
    dils = np.repeat(np.array([d for _, d in A_GROUPS], np.float32), A_HEADS)
    return slopes * dils


def _trunk(x, mem, weights):
    (g_norm, g_mem, w_nat, w_dil, w_mem, table, coef, wa, wb, wm, wo, g_final) = weights
    proj_nat, proj_dil = _fused_proj(x, g_norm, w_nat, w_dil)
    mem_kv = _norm_proj(mem, g_mem, w_mem, dil=1, tn=1024)
    ga = _dilated_mixer(coef, proj_nat, proj_dil)
    gb = _neighbour_mixer(proj_nat, table)
    gm = _memory_mixer(proj_nat, mem_kv)
    return _merge_out(ga, gb, gm, proj_nat, x, wa, wb, wm, wo, g_final)


def kernel(x_prompt, x_sample, mem_prompt, mem_sample, norm_gain, mem_norm_gain, w_in, w_mem_kv,
           rpb, w_proj_a, w_proj_b, w_proj_m, w_out, final_norm_gain):
    assert norm_gain.shape[0] == 1, "single-layer trunk"
    w_nat, w_dil = _prepare_weights(w_in[0])
    weights = (norm_gain[0], mem_norm_gain[0], w_nat, w_dil, w_mem_kv[0].astype(BF16),
               _neighbour_table(rpb[0]), _alibi_coefficients(),
               w_proj_a[0].astype(BF16), w_proj_b[0].astype(BF16), w_proj_m[0].astype(BF16),
               w_out[0].astype(BF16), final_norm_gain)
    return (_trunk(x_prompt, mem_prompt, weights), _trunk(x_sample, mem_sample, weights))
```

```python
import functools

import numpy as np
import jax
import jax.numpy as jnp
from jax import lax
from jax.experimental import pallas as pl
from jax.experimental.pallas import tpu as pltpu

F32 = jnp.float32
BF16 = jnp.bfloat16

D_MODEL = 1024
SEQ = 2048
N_MEM = 256
GRID_W = 64
ROWS = SEQ // GRID_W
HEAD_DIM = 64
A_GROUPS = ((128, 1), (512, 4), (2048, 16))
A_HEADS = 8
NA_ROWS = 8
NA_COLS = 16
M_HEADS = 4
M_HEAD_DIM = 128
RMS_EPS = 1e-6
NEG_INF = -1e30
LOG2_E = 1.4426950408889634

LANES = 128
N_PAIRS = A_HEADS // 2
QT = 128
STRIDE_STEP = 4
A_REACH = 64
B_KEYS = min(NA_ROWS, ROWS) * GRID_W
M_QT = 256
MEM_SEQS_PER_STEP = 4
A_UNROLL = 8
B_UNROLL = 32
M_UNROLL = 8

NAT_COLS = 8192
COL_A_QKV = 0
COL_B_QKV = 1536
COL_AG = 3072
COL_BG = 3584
COL_MQ = 4096
COL_MG = 4608
COL_MERGE = 5120
DIL_COLS = 1536

VMEM_LIMIT = 52 * 1024 * 1024
PROJ_VMEM_LIMIT = 56 * 1024 * 1024


def _gated_ratio(num, den, gate):
    return (num * gate) / (den * (1.0 + jnp.exp(-gate)))


def _dot_nt(a, b):
    return lax.dot_general(a, b, (((1,), (1,)), ((), ())), preferred_element_type=F32)


def _norm_proj_kernel(x_ref, g_ref, w_ref, o_ref, h_ref, *, dil, seq):
    sub = seq // dil
    rows = min(sub, 256)

    @pl.when(pl.program_id(1) == 0)
    def _():
        g = g_ref[...]
        for r in range(dil):
            for c in range(sub // rows):
                xs = x_ref[0, c * rows:(c + 1) * rows, r * D_MODEL:(r + 1) * D_MODEL]
                ms = jnp.mean(xs * xs, axis=-1, keepdims=True)
                h = (xs * lax.rsqrt(ms + RMS_EPS)) * g
                h_ref[r * sub + c * rows:r * sub + (c + 1) * rows, :] = h.astype(BF16)

    mt = min(seq, 512)
    for c in range(seq // mt):
        acc = jnp.dot(h_ref[c * mt:(c + 1) * mt, :], w_ref[...], preferred_element_type=F32)
        o_ref[0, c * mt:(c + 1) * mt, :] = acc.astype(BF16)


def _norm_proj(x, gain, w, *, dil, tn):
    b, seq, _ = x.shape
    n = w.shape[1]
    sub = seq // dil
    xv = x.reshape(b, sub, dil * D_MODEL)
    return pl.pallas_call(
        functools.partial(_norm_proj_kernel, dil=dil, seq=seq),
        grid=(b, n // tn),
        in_specs=[
            pl.BlockSpec((1, sub, dil * D_MODEL), lambda i, j: (i, 0, 0)),
            pl.BlockSpec((1, D_MODEL), lambda i, j: (0, 0)),
            pl.BlockSpec((D_MODEL, tn), lambda i, j: (0, j)),
        ],
        out_specs=pl.BlockSpec((1, seq, tn), lambda i, j: (i, 0, j)),
        out_shape=jax.ShapeDtypeStruct((b, seq, n), BF16),
        scratch_shapes=[pltpu.VMEM((seq, D_MODEL), BF16)],
        compiler_params=pltpu.CompilerParams(
            dimension_semantics=("arbitrary", "arbitrary"), vmem_limit_bytes=VMEM_LIMIT),
        name=f"norm_proj_d{dil}",
    )(xv, gain.reshape(1, D_MODEL), w)


NAT_TN = 1024
N_NAT_STEPS = NAT_COLS // NAT_TN
DIL_TN = 768
N_DIL_STEPS = DIL_COLS // DIL_TN
PROJ_MT = 512


def _fused_proj_kernel(x_ref, xs_ref, g_ref, gs_ref, wn_ref, wd_ref, on_ref, od_ref,
                       hn_ref, h4_ref, h16_ref, inv_ref, sa_ref, sb_ref):
    assert N_NAT_STEPS == D_MODEL // LANES, "one lane slab is regrouped per natural-order column step"
    j = pl.program_id(1)
    quarter = SEQ // STRIDE_STEP

    @pl.when(j == 0)
    def _():
        g = g_ref[...]
        rows = 256
        for c in range(SEQ // rows):
            sl = slice(c * rows, (c + 1) * rows)
            xs = x_ref[0, sl, :]
            inv = lax.rsqrt(jnp.mean(xs * xs, axis=-1, keepdims=True) + RMS_EPS)
            inv_ref[sl, :] = jnp.broadcast_to(inv, (rows, LANES))
            hn_ref[sl, :] = ((xs * inv) * g).astype(BF16)

    @pl.when(j < N_NAT_STEPS)
    def _():
        assert SEQ // PROJ_MT == STRIDE_STEP
        sa_ref[...] = (xs_ref[0] * inv_ref[...]) * gs_ref[...]
        for c in range(SEQ // PROJ_MT):
            sl = slice(c * PROJ_MT, (c + 1) * PROJ_MT)
            on_ref[0, sl, :] = jnp.dot(hn_ref[sl, :], wn_ref[...], preferred_element_type=F32).astype(BF16)
            part = sa_ref[pl.ds(c, quarter, stride=STRIDE_STEP), :]
            sb_ref[c * quarter:(c + 1) * quarter, :] = part
            h4_ref[j, c * quarter:(c + 1) * quarter, :] = part.astype(BF16)
            for rb in range(STRIDE_STEP):
                slot = c * STRIDE_STEP + rb
                h16_ref[j, slot * QT:(slot + 1) * QT, :] = (
                    sb_ref[pl.ds(c * quarter + rb, QT, stride=STRIDE_STEP), :].astype(BF16))

    for group, h_ref in enumerate((h4_ref, h16_ref)):
        first = N_NAT_STEPS + group * N_DIL_STEPS

        @pl.when((j >= first) & (j < first + N_DIL_STEPS))
        def _(h_ref=h_ref):
            for c in range(SEQ // PROJ_MT):
                sl = slice(c * PROJ_MT, (c + 1) * PROJ_MT)
                h = jnp.concatenate([h_ref[k, sl, :] for k in range(N_NAT_STEPS)], axis=1)
                od_ref[0, 0, sl, :] = jnp.dot(h, wd_ref[0], preferred_element_type=F32).astype(BF16)


def _fused_proj(x, gain, w_nat, w_dil):
    b = x.shape[0]

    def dil_step(j):
        return jnp.clip(j - N_NAT_STEPS, 0, 2 * N_DIL_STEPS - 1)

    return pl.pallas_call(
        _fused_proj_kernel,
        grid=(b, N_NAT_STEPS + 2 * N_DIL_STEPS),
        in_specs=[
            pl.BlockSpec((1, SEQ, D_MODEL), lambda i, j: (i, 0, 0)),
            pl.BlockSpec((1, SEQ, LANES), lambda i, j: (i, 0, jnp.minimum(j, N_NAT_STEPS - 1))),
            pl.BlockSpec((1, D_MODEL), lambda i, j: (0, 0)),
            pl.BlockSpec((1, LANES), lambda i, j: (0, jnp.minimum(j, N_NAT_STEPS - 1))),
            pl.BlockSpec((D_MODEL, NAT_TN), lambda i, j: (0, jnp.minimum(j, N_NAT_STEPS - 1))),
            pl.BlockSpec((1, D_MODEL, DIL_TN), lambda i, j: (dil_step(j), 0, 0)),
        ],
        out_specs=[
            pl.BlockSpec((1, SEQ, NAT_TN), lambda i, j: (i, 0, jnp.minimum(j, N_NAT_STEPS - 1))),
            pl.BlockSpec((1, 1, SEQ, DIL_TN),
                         lambda i, j: (i, dil_step(j) // N_DIL_STEPS, 0, dil_step(j) % N_DIL_STEPS)),
        ],
        out_shape=[jax.ShapeDtypeStruct((b, SEQ, NAT_COLS), BF16),
                   jax.ShapeDtypeStruct((b, 2, SEQ, DIL_COLS), BF16)],
        scratch_shapes=[
            pltpu.VMEM((SEQ, D_MODEL), BF16),
            pltpu.VMEM((D_MODEL // LANES, SEQ, LANES), BF16),
            pltpu.VMEM((D_MODEL // LANES, SEQ, LANES), BF16),
            pltpu.VMEM((SEQ, LANES), F32),
            pltpu.VMEM((SEQ, LANES), F32),
            pltpu.VMEM((SEQ, LANES), F32),
        ],
        compiler_params=pltpu.CompilerParams(
            dimension_semantics=("arbitrary", "arbitrary"), vmem_limit_bytes=PROJ_VMEM_LIMIT),
        name="fused_proj",
    )(x, x, gain.reshape(1, D_MODEL), gain.reshape(1, D_MODEL), w_nat, w_dil)


def _pair_attention(q, k, vext, bias):
    rows = q.shape[0]
    head0 = lax.broadcasted_iota(jnp.int32, (rows, LANES), 1) < HEAD_DIM
    zero = jnp.zeros_like(q)
    q2 = jnp.concatenate([jnp.where(head0, q, zero), jnp.where(head0, zero, q)], axis=0)
    s = _dot_nt(q2, k) + bias
    m = jnp.max(s, axis=-1, keepdims=True)
    p = jnp.exp(s - m).astype(BF16)
    oe = jnp.dot(p, vext, preferred_element_type=F32)
    o = jnp.where(head0, oe[:rows, :LANES], oe[rows:, :LANES])
    l = jnp.where(head0, oe[:rows, LANES:], oe[rows:, LANES:])
    m = jnp.where(head0, m[:rows], m[rows:])
    return o, l, m


def _dilated_kernel(coef_ref, q0_ref, q1_ref, q2_ref, ag_ref, out_ref,
                    vext_ref, bias_ref, o_ref, l_ref, m_ref, stage_ref):
    hp = pl.program_id(0)
    scale = HEAD_DIM ** -0.5
    tk = 2 * QT
    n_tiles = SEQ // QT
    quarter = SEQ // STRIDE_STEP
    qkv_refs = (q0_ref, q1_ref, q2_ref)
    g_staged = [d for _, d in A_GROUPS].index(STRIDE_STEP ** 2)

    @pl.when(pl.program_id(1) == 0)
    def _():
        row = lax.broadcasted_iota(jnp.int32, (QT, tk), 0)
        col = lax.broadcasted_iota(jnp.int32, (QT, tk), 1)
        for g, (_, dil) in enumerate(A_GROUPS):
            vext_ref[g, :, LANES:] = jnp.ones((SEQ, LANES), BF16)
            if SEQ // dil > QT:
                variants = ((0, 0, tk), (-A_REACH, 0, tk), (-2 * A_REACH, 0, tk))
            else:
                variants = ((0, 0, QT), (-QT, QT, tk))
            for v, (off, lo, hi) in enumerate(variants):
                rel = jnp.abs(col + off - row)
                relf = rel.astype(F32)
                masked = (rel > A_REACH) | (col < lo) | (col >= hi)
                for hh in range(2):
                    c = coef_ref[g * A_HEADS + 2 * hp + hh]
                    bias_ref[g, v, hh * QT:(hh + 1) * QT, :] = jnp.where(masked, NEG_INF, -c * relf)

    for g, qkv_ref in enumerate(qkv_refs):
        vext_ref[g, :, :LANES] = qkv_ref[:, 2 * LANES:3 * LANES]

    def group_tile(u, g):
        dil = A_GROUPS[g][1]
        qkv_ref = qkv_refs[g]
        sub = SEQ // dil
        nt = sub // QT
        if nt == 1:
            r, t, vidx = u, 0, u % 2
            ks = (u % 2) * -QT
        else:
            r, t = u // nt, u % nt
            ks = jnp.clip(t * QT - A_REACH, 0, sub - tk)
            vidx = jnp.where(t == 0, 0, jnp.where(t == nt - 1, 2, 1))
        qrow = pl.multiple_of(r * sub + t * QT, QT)
        krow = pl.multiple_of(r * sub + ks, A_REACH)
        q = qkv_ref[pl.ds(qrow, QT), 0:LANES] * jnp.asarray(scale, BF16)
        k = qkv_ref[pl.ds(krow, tk), LANES:2 * LANES]
        vext = vext_ref[g, pl.ds(krow, tk), :]
        o, l, m = _pair_attention(q, k, vext, bias_ref[g, vidx])
        if dil == 1:
            dst = pl.ds(pl.multiple_of(t * QT, QT), QT)
        elif dil == STRIDE_STEP:
            dst = pl.ds(r + dil * QT * t, QT, stride=dil)
        else:
            dst = pl.ds((r // STRIDE_STEP) * quarter + r % STRIDE_STEP, QT, stride=STRIDE_STEP)
        if dil == STRIDE_STEP ** 2:
            stage_ref[0, dst, :] = o
            stage_ref[1, dst, :] = l
            stage_ref[2, dst, :] = m
        else:
            o_ref[g, dst, :] = o
            l_ref[g, dst, :] = l
            m_ref[g, dst, :] = m

    def tiles(u, carry):
        for g in range(len(A_GROUPS)):
            group_tile(u, g)
        return carry

    lax.fori_loop(0, n_tiles, tiles, 0, unroll=A_UNROLL)

    def interleave(ra, carry):
        src = pl.ds(pl.multiple_of(ra * quarter, quarter), quarter)
        for k, dst_ref in enumerate((o_ref, l_ref, m_ref)):
            dst_ref[g_staged, pl.ds(ra, quarter, stride=STRIDE_STEP), :] = stage_ref[k, src, :]
        return carry

    lax.fori_loop(0, STRIDE_STEP, interleave, 0)

    def combine(u, carry):
        sl = pl.ds(pl.multiple_of(u * QT, QT), QT)
        ms = [m_ref[g, sl, :] for g in range(3)]
        mm = jnp.maximum(jnp.maximum(ms[0], ms[1]), ms[2])
        ws = [jnp.exp(m - mm) for m in ms]
        num = ws[0] * o_ref[0, sl, :] + ws[1] * o_ref[1, sl, :] + ws[2] * o_ref[2, sl, :]
        den = ws[0] * l_ref[0, sl, :] + ws[1] * l_ref[1, sl, :] + ws[2] * l_ref[2, sl, :]
        gate = ag_ref[0, sl, :].astype(F32)
        out_ref[0, sl, :] = _gated_ratio(num, den, gate).astype(BF16)
        return carry

    lax.fori_loop(0, SEQ // QT, combine, 0, unroll=4)


def _dilated_mixer(coef, proj_nat, proj_dil):
    b = proj_nat.shape[0]
    grid_spec = pltpu.PrefetchScalarGridSpec(
        num_scalar_prefetch=1,
        grid=(N_PAIRS, b),
        in_specs=[
            pl.BlockSpec((None, SEQ, 3 * LANES), lambda j, i, c: (i, 0, COL_A_QKV // (3 * LANES) + j)),
            pl.BlockSpec((None, None, SEQ, 3 * LANES), lambda j, i, c: (i, 0, 0, j)),
            pl.BlockSpec((None, None, SEQ, 3 * LANES), lambda j, i, c: (i, 1, 0, j)),
            pl.BlockSpec((1, SEQ, LANES), lambda j, i, c: (i, 0, COL_AG // LANES + j)),
        ],
        out_specs=pl.BlockSpec((1, SEQ, LANES), lambda j, i, c: (i, 0, j)),
        scratch_shapes=[
            pltpu.VMEM((3, SEQ, 2 * LANES), BF16),
            pltpu.VMEM((3, 3, 2 * QT, 2 * QT), F32),
            pltpu.VMEM((3, SEQ, LANES), F32),
            pltpu.VMEM((3, SEQ, LANES), F32),
            pltpu.VMEM((3, SEQ, LANES), F32),
            pltpu.VMEM((3, SEQ, LANES), F32),
        ],
    )
    return pl.pallas_call(
        _dilated_kernel,
        grid_spec=grid_spec,
        out_shape=jax.ShapeDtypeStruct((b, SEQ, N_PAIRS * LANES), BF16),
        compiler_params=pltpu.CompilerParams(
            dimension_semantics=("arbitrary", "arbitrary"), vmem_limit_bytes=VMEM_LIMIT),
        name="dilated_mixer",
    )(coef, proj_nat, proj_dil, proj_dil, proj_nat)


def _neighbour_kernel(qkv_ref, bg_ref, tab_ref, out_ref, vext_ref):
    vext_ref[:, LANES:] = jnp.ones((SEQ, LANES), BF16)
    vext_ref[:, :LANES] = qkv_ref[0, :, 2 * LANES:3 * LANES]
    scale = HEAD_DIM ** -0.5
    wr = min(NA_ROWS, ROWS)

    def tile(r, carry):
        r0 = jnp.clip(r - wr // 2, 0, ROWS - wr)
        krow = pl.multiple_of(r0 * GRID_W, GRID_W)
        sl = pl.ds(pl.multiple_of(r * GRID_W, GRID_W), GRID_W)
        q = qkv_ref[0, sl, 0:LANES] * jnp.asarray(scale, BF16)
        k = qkv_ref[0, pl.ds(krow, B_KEYS), LANES:2 * LANES]
        vext = vext_ref[pl.ds(krow, B_KEYS), :]
        o, l, _ = _pair_attention(q, k, vext, tab_ref[0, r - r0])
        gate = bg_ref[0, sl, :].astype(F32)
        out_ref[0, sl, :] = _gated_ratio(o, l, gate).astype(BF16)
        return carry

    lax.fori_loop(0, ROWS, tile, 0, unroll=B_UNROLL)


def _skew(a, n_rows):
    period = a.shape[-1]
    reps = -(-n_rows * (period + 1) // period)
    flat = jnp.tile(a, (1,) * (a.ndim - 1) + (reps,))[..., :n_rows * (period + 1)]
    return flat.reshape(*a.shape[:-1], n_rows, period + 1)


def _neighbour_table(rpb):
    wr = min(NA_ROWS, ROWS)
    pad = GRID_W - NA_COLS
    rpb_pad = jnp.pad(rpb, ((0, 0), (0, 0), (pad, pad)))
    toeplitz = _skew(rpb_pad, GRID_W)[:, :, ::-1, :GRID_W]
    by_row = _skew(toeplitz.transpose(0, 2, 3, 1), wr)[..., ::-1, :wr]
    tab = by_row.transpose(0, 3, 1, 4, 2).reshape(A_HEADS, wr, GRID_W, B_KEYS)
    c = np.arange(GRID_W)
    kc = np.arange(B_KEYS) % GRID_W
    sc = np.clip(c - NA_COLS // 2, 0, GRID_W - NA_COLS)
    col_ok = (kc[None, :] >= sc[:, None]) & (kc[None, :] < sc[:, None] + NA_COLS)
    tab = jnp.where(col_ok[None, None], tab, NEG_INF)
    tab = tab.reshape(N_PAIRS, 2, wr, GRID_W, B_KEYS).transpose(0, 2, 1, 3, 4)
    return tab.reshape(N_PAIRS, wr, 2 * GRID_W, B_KEYS)


def _neighbour_mixer(proj_nat, table):
    b = proj_nat.shape[0]
    return pl.pallas_call(
        _neighbour_kernel,
        grid=(N_PAIRS, b),
        in_specs=[
            pl.BlockSpec((1, SEQ, 3 * LANES), lambda j, i: (i, 0, COL_B_QKV // (3 * LANES) + j)),
            pl.BlockSpec((1, SEQ, LANES), lambda j, i: (i, 0, COL_BG // LANES + j)),
            pl.BlockSpec((1, min(NA_ROWS, ROWS), 2 * GRID_W, B_KEYS), lambda j, i: (j, 0, 0, 0)),
        ],
        out_specs=pl.BlockSpec((1, SEQ, LANES), lambda j, i: (i, 0, j)),
        out_shape=jax.ShapeDtypeStruct((b, SEQ, N_PAIRS * LANES), BF16),
        scratch_shapes=[pltpu.VMEM((SEQ, 2 * LANES), BF16)],
        compiler_params=pltpu.CompilerParams(
            dimension_semantics=("arbitrary", "arbitrary"), vmem_limit_bytes=VMEM_LIMIT),
        name="neighbour_mixer",
    )(proj_nat, proj_nat, table)


def _memory_kernel(q_ref, mg_ref, mkv_ref, out_ref, vext_ref):
    width = M_HEADS * M_HEAD_DIM
    scale = M_HEAD_DIM ** -0.5
    for h in range(M_HEADS):
        vext_ref[h, :, LANES:] = jnp.ones((N_MEM, LANES), BF16)
        vext_ref[h, :, :LANES] = mkv_ref[0, :, width + h * LANES:width + (h + 1) * LANES]

    def tile(t, carry):
        sl = pl.ds(pl.multiple_of(t * M_QT, M_QT), M_QT)
        for h in range(M_HEADS):
            cols = slice(h * LANES, (h + 1) * LANES)
            s = _dot_nt(q_ref[0, sl, cols], mkv_ref[0, :, cols])
            m = jnp.max(s, axis=-1, keepdims=True)
            p = jnp.exp2((s - m) * (scale * LOG2_E)).astype(BF16)
            oe = jnp.dot(p, vext_ref[h], preferred_element_type=F32)
            gate = mg_ref[0, sl, cols].astype(F32)
            out_ref[0, sl, cols] = _gated_ratio(oe[:, :LANES], oe[:, LANES:], gate).astype(BF16)
        return carry

    lax.fori_loop(0, SEQ // M_QT, tile, 0, unroll=M_UNROLL)


def _memory_mixer(proj_nat, mem_kv):
    b = proj_nat.shape[0]
    width = M_HEADS * M_HEAD_DIM
    assert M_HEAD_DIM == LANES and COL_MQ % width == 0 and COL_MG % width == 0
    return pl.pallas_call(
        _memory_kernel,
        grid=(b,),
        in_specs=[
            pl.BlockSpec((1, SEQ, width), lambda i: (i, 0, COL_MQ // width)),
            pl.BlockSpec((1, SEQ, width), lambda i: (i, 0, COL_MG // width)),
            pl.BlockSpec((1, N_MEM, 2 * width), lambda i: (i, 0, 0)),
        ],
        out_specs=pl.BlockSpec((1, SEQ, width), lambda i: (i, 0, 0)),
        out_shape=jax.ShapeDtypeStruct((b, SEQ, width), BF16),
        scratch_shapes=[pltpu.VMEM((M_HEADS, N_MEM, 2 * LANES), BF16)],
        compiler_params=pltpu.CompilerParams(
            dimension_semantics=("arbitrary",), vmem_limit_bytes=VMEM_LIMIT),
        name="memory_mixer",
    )(proj_nat, proj_nat, mem_kv)


def _merge_out_kernel(ga_ref, gb_ref, gm_ref, s0_ref, s1_ref, s2_ref, x_ref,
                      wa_ref, wb_ref, wm_ref, wo_ref, g_ref, out_ref, *, chunk):
    for c in range(out_ref.shape[1] // chunk):
        rows = slice(c * chunk, (c + 1) * chunk)
        merged = None
        for br_ref, w_ref, s_ref in ((ga_ref, wa_ref, s0_ref), (gb_ref, wb_ref, s1_ref),
                                     (gm_ref, wm_ref, s2_ref)):
            branch = jnp.dot(br_ref[0, rows, :], w_ref[...], preferred_element_type=F32)
            term = jax.nn.sigmoid(s_ref[0, rows, :].astype(F32)) * branch
            merged = term if merged is None else merged + term
        y = x_ref[0, rows, :] + jnp.dot(merged.astype(BF16), wo_ref[...], preferred_element_type=F32)
        ms = jnp.mean(y * y, axis=-1, keepdims=True)
        out_ref[0, rows, :] = (y * lax.rsqrt(ms + RMS_EPS)) * g_ref[...]


def _merge_out(ga, gb, gm, proj_nat, x, wa, wb, wm, wo, gain, *, tm=1024, chunk=512):
    b = x.shape[0]
    width = ga.shape[-1]
    br_spec = pl.BlockSpec((1, tm, width), lambda i, j: (i, j, 0))
    w_spec = pl.BlockSpec((width, D_MODEL), lambda i, j: (0, 0))
    gate_specs = [
        pl.BlockSpec((1, tm, D_MODEL), functools.partial(lambda i, j, n: (i, j, COL_MERGE // D_MODEL + n), n=n))
        for n in range(3)
    ]
    return pl.pallas_call(
        functools.partial(_merge_out_kernel, chunk=chunk),
        grid=(b, SEQ // tm),
        in_specs=[br_spec, br_spec, br_spec, *gate_specs,
                  pl.BlockSpec((1, tm, D_MODEL), lambda i, j: (i, j, 0)),
                  w_spec, w_spec, w_spec,
                  pl.BlockSpec((D_MODEL, D_MODEL), lambda i, j: (0, 0)),
                  pl.BlockSpec((1, D_MODEL), lambda i, j: (0, 0))],
        out_specs=pl.BlockSpec((1, tm, D_MODEL), lambda i, j: (i, j, 0)),
        out_shape=jax.ShapeDtypeStruct((b, SEQ, D_MODEL), F32),
        compiler_params=pltpu.CompilerParams(
            dimension_semantics=("arbitrary", "arbitrary"), vmem_limit_bytes=VMEM_LIMIT),
        name="merge_out",
    )(ga, gb, gm, proj_nat, proj_nat, proj_nat, x, wa, wb, wm, wo, gain.reshape(1, D_MODEL))


def _prepare_weights(w_in):
    a_w = 3 * A_HEADS * HEAD_DIM
    b_w = A_HEADS * HEAD_DIM
    aq, ak, av = (w_in[:, i * a_w:(i + 1) * a_w].reshape(D_MODEL, 3, N_PAIRS, LANES) for i in range(3))
    a_qkv = jnp.stack([aq, ak, av], axis=3)
    off = 3 * a_w
    ag = w_in[:, off:off + b_w]
    off += b_w
    bq, bk, bv = (w_in[:, off + i * b_w:off + (i + 1) * b_w].reshape(D_MODEL, N_PAIRS, LANES) for i in range(3))
    b_qkv = jnp.stack([bq, bk, bv], axis=2).reshape(D_MODEL, 3 * b_w)
    off += 3 * b_w
    bg = w_in[:, off:off + b_w]
    rest = w_in[:, off + b_w:]
    w_nat = jnp.concatenate([a_qkv[:, 0].reshape(D_MODEL, a_w), b_qkv, ag, bg, rest], axis=1)
    w_dil = a_qkv[:, 1:].reshape(D_MODEL, 2 * N_DIL_STEPS, DIL_TN).transpose(1, 0, 2)
    return w_nat.astype(BF16), w_dil.astype(BF16)


def _alibi_coefficients():
    n_heads = len(A_GROUPS) * A_HEADS
    slopes = jnp.exp2(-8.0 * jnp.arange(1, n_heads + 1, dtype=F32) / n_heads)
    dils = np.repeat(np.array([d for _, d in A_GROUPS], np.float32), A_HEADS)
    return slopes * dils


def _trunk(x, mem, weights):
    (g_norm, g_mem, w_nat, w_dil, w_mem, table, coef, wa, wb, wm, wo, g_final) = weights
    proj_nat, proj_dil = _fused_proj(x, g_norm, w_nat, w_dil)
    b = mem.shape[0]
    assert b % MEM_SEQS_PER_STEP == 0
    mem_rows = mem.reshape(b // MEM_SEQS_PER_STEP, MEM_SEQS_PER_STEP * N_MEM, D_MODEL)
    mem_kv = _norm_proj(mem_rows, g_mem, w_mem, dil=1, tn=1024).reshape(b, N_MEM, w_mem.shape[1])
    ga = _dilated_mixer(coef, proj_nat, proj_dil)
    gb = _neighbour_mixer(proj_nat, table)
    gm = _memory_mixer(proj_nat, mem_kv)
    return _merge_out(ga, gb, gm, proj_nat, x, wa, wb, wm, wo, g_final)


def kernel(x_prompt, x_sample, mem_prompt, mem_sample, norm_gain, mem_norm_gain, w_in, w_mem_kv,
           rpb, w_proj_a, w_proj_b, w_proj_m, w_out, final_norm_gain):
    assert norm_gain.shape[0] == 1, "single-layer trunk"
    w_nat, w_dil = _prepare_weights(w_in[0])
    weights = (norm_gain[0], mem_norm_gain[0], w_nat, w_dil, w_mem_kv[0].astype(BF16),
               _neighbour_table(rpb[0]), _alibi_coefficients(),
               w_proj_a[0].astype(BF16), w_proj_b[0].astype(BF16), w_proj_m[0].astype(BF16),
               w_out[0].astype(BF16), final_norm_gain)
    return (_trunk(x_prompt, mem_prompt, weights), _trunk(x_sample, mem_sample, weights))
```

```python
import functools

import numpy as np
import jax
import jax.numpy as jnp
from jax import lax
from jax.experimental import pallas as pl
from jax.experimental.pallas import tpu as pltpu

F32 = jnp.float32
BF16 = jnp.bfloat16

D_MODEL = 1024
SEQ = 2048
N_MEM = 256
GRID_W = 64
ROWS = SEQ // GRID_W
HEAD_DIM = 64
A_GROUPS = ((128, 1), (512, 4), (2048, 16))
A_HEADS = 8
NA_ROWS = 8
NA_COLS = 16
M_HEADS = 4
M_HEAD_DIM = 128
RMS_EPS = 1e-6
NEG_INF = -1e30
LOG2_E = 1.4426950408889634

LANES = 128
N_PAIRS = A_HEADS // 2
QT = 128
STRIDE_STEP = 4
A_REACH = 64
B_KEYS = min(NA_ROWS, ROWS) * GRID_W
M_QT = 256
MEM_SEQS_PER_STEP = 4
A_UNROLL = 8
B_UNROLL = 32
M_UNROLL = 8

NAT_COLS = 8192
COL_A_QKV = 0
COL_B_QKV = 1536
COL_AG = 3072
COL_BG = 3584
COL_MQ = 4096
COL_MG = 4608
COL_MERGE = 5120
DIL_COLS = 1536

VMEM_LIMIT = 52 * 1024 * 1024
PROJ_VMEM_LIMIT = 56 * 1024 * 1024


def _gated_ratio(num, den, gate):
    return (num * gate) / (den * (1.0 + jnp.exp(-gate)))


def _dot_nt(a, b):
    return lax.dot_general(a, b, (((1,), (1,)), ((), ())), preferred_element_type=F32)


def _norm_proj_kernel(x_ref, g_ref, w_ref, o_ref, h_ref, *, dil, seq):
    sub = seq // dil
    rows = min(sub, 256)

    @pl.when(pl.program_id(1) == 0)
    def _():
        g = g_ref[...]
        for r in range(dil):
            for c in range(sub // rows):
                xs = x_ref[0, c * rows:(c + 1) * rows, r * D_MODEL:(r + 1) * D_MODEL]
                ms = jnp.mean(xs * xs, axis=-1, keepdims=True)
                h = (xs * lax.rsqrt(ms + RMS_EPS)) * g
                h_ref[r * sub + c * rows:r * sub + (c + 1) * rows, :] = h.astype(BF16)

    mt = min(seq, 512)
    for c in range(seq // mt):
        acc = jnp.dot(h_ref[c * mt:(c + 1) * mt, :], w_ref[...], preferred_element_type=F32)
        o_ref[0, c * mt:(c + 1) * mt, :] = acc.astype(BF16)


def _norm_proj(x, gain, w, *, dil, tn):
    b, seq, _ = x.shape
    n = w.shape[1]
    sub = seq // dil
    xv = x.reshape(b, sub, dil * D_MODEL)
    return pl.pallas_call(
        functools.partial(_norm_proj_kernel, dil=dil, seq=seq),
        grid=(b, n // tn),
        in_specs=[
            pl.BlockSpec((1, sub, dil * D_MODEL), lambda i, j: (i, 0, 0)),
            pl.BlockSpec((1, D_MODEL), lambda i, j: (0, 0)),
            pl.BlockSpec((D_MODEL, tn), lambda i, j: (0, j)),
        ],
        out_specs=pl.BlockSpec((1, seq, tn), lambda i, j: (i, 0, j)),
        out_shape=jax.ShapeDtypeStruct((b, seq, n), BF16),
        scratch_shapes=[pltpu.VMEM((seq, D_MODEL), BF16)],
        compiler_params=pltpu.CompilerParams(
            dimension_semantics=("arbitrary", "arbitrary"), vmem_limit_bytes=VMEM_LIMIT),
        name=f"norm_proj_d{dil}",
    )(xv, gain.reshape(1, D_MODEL), w)


NAT_TN = 1024
N_NAT_STEPS = NAT_COLS // NAT_TN
DIL_TN = 768
N_DIL_STEPS = DIL_COLS // DIL_TN
PROJ_MT = 512


def _fused_proj_kernel(x_ref, xs_ref, g_ref, gs_ref, wn_ref, wd_ref, on_ref, od_ref,
                       hn_ref, h4_ref, h16_ref, inv_ref, sa_ref, sb_ref):
    assert N_NAT_STEPS == D_MODEL // LANES, "one lane slab is regrouped per natural-order column step"
    j = pl.program_id(1)
    quarter = SEQ // STRIDE_STEP

    @pl.when(j == 0)
    def _():
        g = g_ref[...]
        rows = 256
        for c in range(SEQ // rows):
            sl = slice(c * rows, (c + 1) * rows)
            xs = x_ref[0, sl, :]
            inv = lax.rsqrt(jnp.mean(xs * xs, axis=-1, keepdims=True) + RMS_EPS)
            inv_ref[sl, :] = jnp.broadcast_to(inv, (rows, LANES))
            hn_ref[sl, :] = ((xs * inv) * g).astype(BF16)

    @pl.when(j < N_NAT_STEPS)
    def _():
        assert SEQ // PROJ_MT == STRIDE_STEP
        sa_ref[...] = (xs_ref[0] * inv_ref[...]) * gs_ref[...]
        for c in range(SEQ // PROJ_MT):
            sl = slice(c * PROJ_MT, (c + 1) * PROJ_MT)
            on_ref[0, sl, :] = jnp.dot(hn_ref[sl, :], wn_ref[...], preferred_element_type=F32).astype(BF16)
            part = sa_ref[pl.ds(c, quarter, stride=STRIDE_STEP), :]
            sb_ref[c * quarter:(c + 1) * quarter, :] = part
            h4_ref[j, c * quarter:(c + 1) * quarter, :] = part.astype(BF16)
            for rb in range(STRIDE_STEP):
                slot = c * STRIDE_STEP + rb
                h16_ref[j, slot * QT:(slot + 1) * QT, :] = (
                    sb_ref[pl.ds(c * quarter + rb, QT, stride=STRIDE_STEP), :].astype(BF16))

    for group, h_ref in enumerate((h4_ref, h16_ref)):
        first = N_NAT_STEPS + group * N_DIL_STEPS

        @pl.when((j >= first) & (j < first + N_DIL_STEPS))
        def _(h_ref=h_ref):
            for c in range(SEQ // PROJ_MT):
                sl = slice(c * PROJ_MT, (c + 1) * PROJ_MT)
                h = jnp.concatenate([h_ref[k, sl, :] for k in range(N_NAT_STEPS)], axis=1)
                od_ref[0, 0, sl, :] = jnp.dot(h, wd_ref[0], preferred_element_type=F32).astype(BF16)


def _fused_proj(x, gain, w_nat, w_dil):
    b = x.shape[0]

    def dil_step(j):
        return jnp.clip(j - N_NAT_STEPS, 0, 2 * N_DIL_STEPS - 1)

    return pl.pallas_call(
        _fused_proj_kernel,
        grid=(b, N_NAT_STEPS + 2 * N_DIL_STEPS),
        in_specs=[
            pl.BlockSpec((1, SEQ, D_MODEL), lambda i, j: (i, 0, 0)),
            pl.BlockSpec((1, SEQ, LANES), lambda i, j: (i, 0, jnp.minimum(j, N_NAT_STEPS - 1))),
            pl.BlockSpec((1, D_MODEL), lambda i, j: (0, 0)),
            pl.BlockSpec((1, LANES), lambda i, j: (0, jnp.minimum(j, N_NAT_STEPS - 1))),
            pl.BlockSpec((D_MODEL, NAT_TN), lambda i, j: (0, jnp.minimum(j, N_NAT_STEPS - 1))),
            pl.BlockSpec((1, D_MODEL, DIL_TN), lambda i, j: (dil_step(j), 0, 0)),
        ],
        out_specs=[
            pl.BlockSpec((1, SEQ, NAT_TN), lambda i, j: (i, 0, jnp.minimum(j, N_NAT_STEPS - 1))),
            pl.BlockSpec((1, 1, SEQ, DIL_TN),
                         lambda i, j: (i, dil_step(j) // N_DIL_STEPS, 0, dil_step(j) % N_DIL_STEPS)),
        ],
        out_shape=[jax.ShapeDtypeStruct((b, SEQ, NAT_COLS), BF16),
                   jax.ShapeDtypeStruct((b, 2, SEQ, DIL_COLS), BF16)],
        scratch_shapes=[
            pltpu.VMEM((SEQ, D_MODEL), BF16),
            pltpu.VMEM((D_MODEL // LANES, SEQ, LANES), BF16),
            pltpu.VMEM((D_MODEL // LANES, SEQ, LANES), BF16),
            pltpu.VMEM((SEQ, LANES), F32),
            pltpu.VMEM((SEQ, LANES), F32),
            pltpu.VMEM((SEQ, LANES), F32),
        ],
        compiler_params=pltpu.CompilerParams(
            dimension_semantics=("arbitrary", "arbitrary"), vmem_limit_bytes=PROJ_VMEM_LIMIT),
        name="fused_proj",
    )(x, x, gain.reshape(1, D_MODEL), gain.reshape(1, D_MODEL), w_nat, w_dil)


def _pair_attention(q, k, vext, bias):
    rows = q.shape[0]
    head0 = lax.broadcasted_iota(jnp.int32, (rows, LANES), 1) < HEAD_DIM
    zero = jnp.zeros_like(q)
    q2 = jnp.concatenate([jnp.where(head0, q, zero), jnp.where(head0, zero, q)], axis=0)
    s = _dot_nt(q2, k) + bias
    m = jnp.max(s, axis=-1, keepdims=True)
    p = jnp.exp2(s - m).astype(BF16)
    oe = jnp.dot(p, vext, preferred_element_type=F32)
    o = jnp.where(head0, oe[:rows, :LANES], oe[rows:, :LANES])
    l = jnp.where(head0, oe[:rows, LANES:], oe[rows:, LANES:])
    m = jnp.where(head0, m[:rows], m[rows:])
    return o, l, m


def _dilated_kernel(coef_ref, q0_ref, q1_ref, q2_ref, ag_ref, out_ref,
                    vext_ref, bias_ref, o_ref, l_ref, m_ref, stage_ref):
    hp = pl.program_id(0)
    tk = 2 * QT
    n_tiles = SEQ // QT
    quarter = SEQ // STRIDE_STEP
    qkv_refs = (q0_ref, q1_ref, q2_ref)
    g_staged = [d for _, d in A_GROUPS].index(STRIDE_STEP ** 2)

    @pl.when(pl.program_id(1) == 0)
    def _():
        row = lax.broadcasted_iota(jnp.int32, (QT, tk), 0)
        col = lax.broadcasted_iota(jnp.int32, (QT, tk), 1)
        for g, (_, dil) in enumerate(A_GROUPS):
            vext_ref[g, :, LANES:] = jnp.ones((SEQ, LANES), BF16)
            if SEQ // dil > QT:
                variants = ((0, 0, tk), (-A_REACH, 0, tk), (-2 * A_REACH, 0, tk))
            else:
                variants = ((0, 0, QT), (-QT, QT, tk))
            for v, (off, lo, hi) in enumerate(variants):
                rel = jnp.abs(col + off - row)
                relf = rel.astype(F32)
                masked = (rel > A_REACH) | (col < lo) | (col >= hi)
                for hh in range(2):
                    c = coef_ref[g * A_HEADS + 2 * hp + hh]
                    bias_ref[g, v, hh * QT:(hh + 1) * QT, :] = jnp.where(masked, NEG_INF, -c * relf)

    for g, qkv_ref in enumerate(qkv_refs):
        vext_ref[g, :, :LANES] = qkv_ref[:, 2 * LANES:3 * LANES]

    def group_tile(u, g):
        dil = A_GROUPS[g][1]
        qkv_ref = qkv_refs[g]
        sub = SEQ // dil
        nt = sub // QT
        if nt == 1:
            r, t, vidx = u, 0, u % 2
            ks = (u % 2) * -QT
        else:
            r, t = u // nt, u % nt
            ks = jnp.clip(t * QT - A_REACH, 0, sub - tk)
            vidx = jnp.where(t == 0, 0, jnp.where(t == nt - 1, 2, 1))
        qrow = pl.multiple_of(r * sub + t * QT, QT)
        krow = pl.multiple_of(r * sub + ks, A_REACH)
        q = qkv_ref[pl.ds(qrow, QT), 0:LANES]
        k = qkv_ref[pl.ds(krow, tk), LANES:2 * LANES]
        vext = vext_ref[g, pl.ds(krow, tk), :]
        o, l, m = _pair_attention(q, k, vext, bias_ref[g, vidx])
        if dil == 1:
            dst = pl.ds(pl.multiple_of(t * QT, QT), QT)
        elif dil == STRIDE_STEP:
            dst = pl.ds(r + dil * QT * t, QT, stride=dil)
        else:
            dst = pl.ds((r // STRIDE_STEP) * quarter + r % STRIDE_STEP, QT, stride=STRIDE_STEP)
        if dil == STRIDE_STEP ** 2:
            stage_ref[0, dst, :] = o
            stage_ref[1, dst, :] = l
            stage_ref[2, dst, :] = m
        else:
            o_ref[g, dst, :] = o
            l_ref[g, dst, :] = l
            m_ref[g, dst, :] = m

    def tiles(u, carry):
        for g in range(len(A_GROUPS)):
            group_tile(u, g)
        return carry

    lax.fori_loop(0, n_tiles, tiles, 0, unroll=A_UNROLL)

    def interleave(ra, carry):
        src = pl.ds(pl.multiple_of(ra * quarter, quarter), quarter)
        for k, dst_ref in enumerate((o_ref, l_ref, m_ref)):
            dst_ref[g_staged, pl.ds(ra, quarter, stride=STRIDE_STEP), :] = stage_ref[k, src, :]
        return carry

    lax.fori_loop(0, STRIDE_STEP, interleave, 0)

    def combine(u, carry):
        sl = pl.ds(pl.multiple_of(u * QT, QT), QT)
        ms = [m_ref[g, sl, :] for g in range(3)]
        mm = jnp.maximum(jnp.maximum(ms[0], ms[1]), ms[2])
        ws = [jnp.exp2(m - mm) for m in ms]
        num = ws[0] * o_ref[0, sl, :] + ws[1] * o_ref[1, sl, :] + ws[2] * o_ref[2, sl, :]
        den = ws[0] * l_ref[0, sl, :] + ws[1] * l_ref[1, sl, :] + ws[2] * l_ref[2, sl, :]
        gate = ag_ref[0, sl, :].astype(F32)
        out_ref[0, sl, :] = _gated_ratio(num, den, gate).astype(BF16)
        return carry

    lax.fori_loop(0, SEQ // QT, combine, 0, unroll=4)


def _dilated_mixer(coef, proj_nat, proj_dil):
    b = proj_nat.shape[0]
    grid_spec = pltpu.PrefetchScalarGridSpec(
        num_scalar_prefetch=1,
        grid=(N_PAIRS, b),
        in_specs=[
            pl.BlockSpec((None, SEQ, 3 * LANES), lambda j, i, c: (i, 0, COL_A_QKV // (3 * LANES) + j)),
            pl.BlockSpec((None, None, SEQ, 3 * LANES), lambda j, i, c: (i, 0, 0, j)),
            pl.BlockSpec((None, None, SEQ, 3 * LANES), lambda j, i, c: (i, 1, 0, j)),
            pl.BlockSpec((1, SEQ, LANES), lambda j, i, c: (i, 0, COL_AG // LANES + j)),
        ],
        out_specs=pl.BlockSpec((1, SEQ, LANES), lambda j, i, c: (i, 0, j)),
        scratch_shapes=[
            pltpu.VMEM((3, SEQ, 2 * LANES), BF16),
            pltpu.VMEM((3, 3, 2 * QT, 2 * QT), F32),
            pltpu.VMEM((3, SEQ, LANES), F32),
            pltpu.VMEM((3, SEQ, LANES), F32),
            pltpu.VMEM((3, SEQ, LANES), F32),
            pltpu.VMEM((3, SEQ, LANES), F32),
        ],
    )
    return pl.pallas_call(
        _dilated_kernel,
        grid_spec=grid_spec,
        out_shape=jax.ShapeDtypeStruct((b, SEQ, N_PAIRS * LANES), BF16),
        compiler_params=pltpu.CompilerParams(
            dimension_semantics=("arbitrary", "arbitrary"), vmem_limit_bytes=VMEM_LIMIT),
        name="dilated_mixer",
    )(coef, proj_nat, proj_dil, proj_dil, proj_nat)


def _neighbour_kernel(qkv_ref, bg_ref, tab_ref, out_ref, vext_ref):
    vext_ref[:, LANES:] = jnp.ones((SEQ, LANES), BF16)
    vext_ref[:, :LANES] = qkv_ref[0, :, 2 * LANES:3 * LANES]
    wr = min(NA_ROWS, ROWS)

    def tile(r, carry):
        r0 = jnp.clip(r - wr // 2, 0, ROWS - wr)
        krow = pl.multiple_of(r0 * GRID_W, GRID_W)
        sl = pl.ds(pl.multiple_of(r * GRID_W, GRID_W), GRID_W)
        q = qkv_ref[0, sl, 0:LANES]
        k = qkv_ref[0, pl.ds(krow, B_KEYS), LANES:2 * LANES]
        vext = vext_ref[pl.ds(krow, B_KEYS), :]
        o, l, _ = _pair_attention(q, k, vext, tab_ref[0, r - r0])
        gate = bg_ref[0, sl, :].astype(F32)
        out_ref[0, sl, :] = _gated_ratio(o, l, gate).astype(BF16)
        return carry

    lax.fori_loop(0, ROWS, tile, 0, unroll=B_UNROLL)


def _skew(a, n_rows):
    period = a.shape[-1]
    reps = -(-n_rows * (period + 1) // period)
    flat = jnp.tile(a, (1,) * (a.ndim - 1) + (reps,))[..., :n_rows * (period + 1)]
    return flat.reshape(*a.shape[:-1], n_rows, period + 1)


def _neighbour_table(rpb):
    wr = min(NA_ROWS, ROWS)
    pad = GRID_W - NA_COLS
    rpb_pad = jnp.pad(rpb, ((0, 0), (0, 0), (pad, pad)))
    toeplitz = _skew(rpb_pad, GRID_W)[:, :, ::-1, :GRID_W]
    by_row = _skew(toeplitz.transpose(0, 2, 3, 1), wr)[..., ::-1, :wr]
    tab = by_row.transpose(0, 3, 1, 4, 2).reshape(A_HEADS, wr, GRID_W, B_KEYS)
    c = np.arange(GRID_W)
    kc = np.arange(B_KEYS) % GRID_W
    sc = np.clip(c - NA_COLS // 2, 0, GRID_W - NA_COLS)
    col_ok = (kc[None, :] >= sc[:, None]) & (kc[None, :] < sc[:, None] + NA_COLS)
    tab = jnp.where(col_ok[None, None], tab * LOG2_E, NEG_INF)
    tab = tab.reshape(N_PAIRS, 2, wr, GRID_W, B_KEYS).transpose(0, 2, 1, 3, 4)
    return tab.reshape(N_PAIRS, wr, 2 * GRID_W, B_KEYS)


def _neighbour_mixer(proj_nat, table):
    b = proj_nat.shape[0]
    return pl.pallas_call(
        _neighbour_kernel,
        grid=(N_PAIRS, b),
        in_specs=[
            pl.BlockSpec((1, SEQ, 3 * LANES), lambda j, i: (i, 0, COL_B_QKV // (3 * LANES) + j)),
            pl.BlockSpec((1, SEQ, LANES), lambda j, i: (i, 0, COL_BG // LANES + j)),
            pl.BlockSpec((1, min(NA_ROWS, ROWS), 2 * GRID_W, B_KEYS), lambda j, i: (j, 0, 0, 0)),
        ],
        out_specs=pl.BlockSpec((1, SEQ, LANES), lambda j, i: (i, 0, j)),
        out_shape=jax.ShapeDtypeStruct((b, SEQ, N_PAIRS * LANES), BF16),
        scratch_shapes=[pltpu.VMEM((SEQ, 2 * LANES), BF16)],
        compiler_params=pltpu.CompilerParams(
            dimension_semantics=("arbitrary", "arbitrary"), vmem_limit_bytes=VMEM_LIMIT),
        name="neighbour_mixer",
    )(proj_nat, proj_nat, table)


def _memory_kernel(q_ref, mg_ref, mkv_ref, out_ref, vext_ref):
    width = M_HEADS * M_HEAD_DIM
    for h in range(M_HEADS):
        vext_ref[h, :, LANES:] = jnp.ones((N_MEM, LANES), BF16)
        vext_ref[h, :, :LANES] = mkv_ref[0, :, width + h * LANES:width + (h + 1) * LANES]

    def tile(t, carry):
        sl = pl.ds(pl.multiple_of(t * M_QT, M_QT), M_QT)
        for h in range(M_HEADS):
            cols = slice(h * LANES, (h + 1) * LANES)
            s = _dot_nt(q_ref[0, sl, cols], mkv_ref[0, :, cols])
            m = jnp.max(s, axis=-1, keepdims=True)
            p = jnp.exp2(s - m).astype(BF16)
            oe = jnp.dot(p, vext_ref[h], preferred_element_type=F32)
            gate = mg_ref[0, sl, cols].astype(F32)
            out_ref[0, sl, cols] = _gated_ratio(oe[:, :LANES], oe[:, LANES:], gate).astype(BF16)
        return carry

    lax.fori_loop(0, SEQ // M_QT, tile, 0, unroll=M_UNROLL)


def _memory_mixer(proj_nat, mem_kv):
    b = proj_nat.shape[0]
    width = M_HEADS * M_HEAD_DIM
    assert M_HEAD_DIM == LANES and COL_MQ % width == 0 and COL_MG % width == 0
    return pl.pallas_call(
        _memory_kernel,
        grid=(b,),
        in_specs=[
            pl.BlockSpec((1, SEQ, width), lambda i: (i, 0, COL_MQ // width)),
            pl.BlockSpec((1, SEQ, width), lambda i: (i, 0, COL_MG // width)),
            pl.BlockSpec((1, N_MEM, 2 * width), lambda i: (i, 0, 0)),
        ],
        out_specs=pl.BlockSpec((1, SEQ, width), lambda i: (i, 0, 0)),
        out_shape=jax.ShapeDtypeStruct((b, SEQ, width), BF16),
        scratch_shapes=[pltpu.VMEM((M_HEADS, N_MEM, 2 * LANES), BF16)],
        compiler_params=pltpu.CompilerParams(
            dimension_semantics=("arbitrary",), vmem_limit_bytes=VMEM_LIMIT),
        name="memory_mixer",
    )(proj_nat, proj_nat, mem_kv)


def _merge_out_kernel(ga_ref, gb_ref, gm_ref, s0_ref, s1_ref, s2_ref, x_ref,
                      wa_ref, wb_ref, wm_ref, wo_ref, g_ref, out_ref, *, chunk):
    for c in range(out_ref.shape[1] // chunk):
        rows = slice(c * chunk, (c + 1) * chunk)
        merged = None
        for br_ref, w_ref, s_ref in ((ga_ref, wa_ref, s0_ref), (gb_ref, wb_ref, s1_ref),
                                     (gm_ref, wm_ref, s2_ref)):
            branch = jnp.dot(br_ref[0, rows, :], w_ref[...], preferred_element_type=F32)
            term = jax.nn.sigmoid(s_ref[0, rows, :].astype(F32)) * branch
            merged = term if merged is None else merged + term
        y = x_ref[0, rows, :] + jnp.dot(merged.astype(BF16), wo_ref[...], preferred_element_type=F32)
        ms = jnp.mean(y * y, axis=-1, keepdims=True)
        out_ref[0, rows, :] = (y * lax.rsqrt(ms + RMS_EPS)) * g_ref[...]


def _merge_out(ga, gb, gm, proj_nat, x, wa, wb, wm, wo, gain, *, tm=1024, chunk=512):
    b = x.shape[0]
    width = ga.shape[-1]
    br_spec = pl.BlockSpec((1, tm, width), lambda i, j: (i, j, 0))
    w_spec = pl.BlockSpec((width, D_MODEL), lambda i, j: (0, 0))
    gate_specs = [
        pl.BlockSpec((1, tm, D_MODEL), functools.partial(lambda i, j, n: (i, j, COL_MERGE // D_MODEL + n), n=n))
        for n in range(3)
    ]
    return pl.pallas_call(
        functools.partial(_merge_out_kernel, chunk=chunk),
        grid=(b, SEQ // tm),
        in_specs=[br_spec, br_spec, br_spec, *gate_specs,
                  pl.BlockSpec((1, tm, D_MODEL), lambda i, j: (i, j, 0)),
                  w_spec, w_spec, w_spec,
                  pl.BlockSpec((D_MODEL, D_MODEL), lambda i, j: (0, 0)),
                  pl.BlockSpec((1, D_MODEL), lambda i, j: (0, 0))],
        out_specs=pl.BlockSpec((1, tm, D_MODEL), lambda i, j: (i, j, 0)),
        out_shape=jax.ShapeDtypeStruct((b, SEQ, D_MODEL), F32),
        compiler_params=pltpu.CompilerParams(
            dimension_semantics=("arbitrary", "arbitrary"), vmem_limit_bytes=VMEM_LIMIT),
        name="merge_out",
    )(ga, gb, gm, proj_nat, proj_nat, proj_nat, x, wa, wb, wm, wo, gain.reshape(1, D_MODEL))


def _prepare_weights(w_in):
    a_w = 3 * A_HEADS * HEAD_DIM
    b_w = A_HEADS * HEAD_DIM
    qk_scale = HEAD_DIM ** -0.5 * LOG2_E
    scaled = (qk_scale, 1.0, 1.0)
    aq, ak, av = ((w_in[:, i * a_w:(i + 1) * a_w] * scaled[i]).reshape(D_MODEL, 3, N_PAIRS, LANES) for i in range(3))
    a_qkv = jnp.stack([aq, ak, av], axis=3)
    off = 3 * a_w
    ag = w_in[:, off:off + b_w]
    off += b_w
    bq, bk, bv = ((w_in[:, off + i * b_w:off + (i + 1) * b_w] * scaled[i]).reshape(D_MODEL, N_PAIRS, LANES)
                  for i in range(3))
    b_qkv = jnp.stack([bq, bk, bv], axis=2).reshape(D_MODEL, 3 * b_w)
    off += 3 * b_w
    bg = w_in[:, off:off + b_w]
    m_w = M_HEADS * M_HEAD_DIM
    mq = w_in[:, off + b_w:off + b_w + m_w] * (M_HEAD_DIM ** -0.5 * LOG2_E)
    rest = w_in[:, off + b_w + m_w:]
    w_nat = jnp.concatenate([a_qkv[:, 0].reshape(D_MODEL, a_w), b_qkv, ag, bg, mq, rest], axis=1)
    w_dil = a_qkv[:, 1:].reshape(D_MODEL, 2 * N_DIL_STEPS, DIL_TN).transpose(1, 0, 2)
    return w_nat.astype(BF16), w_dil.astype(BF16)


def _alibi_coefficients():
    n_heads = len(A_GROUPS) * A_HEADS
    slopes = jnp.exp2(-8.0 * jnp.arange(1, n_heads + 1, dtype=F32) / n_heads)
    dils = np.repeat(np.array([d for _, d in A_GROUPS], np.float32), A_HEADS)
    return slopes * dils * LOG2_E


def _trunk(x, mem, weights):
    (g_norm, g_mem, w_nat, w_dil, w_mem, table, coef, wa, wb, wm, wo, g_final) = weights
    proj_nat, proj_dil = _fused_proj(x, g_norm, w_nat, w_dil)
    b = mem.shape[0]
    assert b % MEM_SEQS_PER_STEP == 0
    mem_rows = mem.reshape(b // MEM_SEQS_PER_STEP, MEM_SEQS_PER_STEP * N_MEM, D_MODEL)
    mem_kv = _norm_proj(mem_rows, g_mem, w_mem, dil=1, tn=1024).reshape(b, N_MEM, w_mem.shape[1])
    ga = _dilated_mixer(coef, proj_nat, proj_dil)
    gb = _neighbour_mixer(proj_nat, table)
    gm = _memory_mixer(proj_nat, mem_kv)
    return _merge_out(ga, gb, gm, proj_nat, x, wa, wb, wm, wo, g_final)


def kernel(x_prompt, x_sample, mem_prompt, mem_sample, norm_gain, mem_norm_gain, w_in, w_mem_kv,
           rpb, w_proj_a, w_proj_b, w_proj_m, w_out, final_norm_gain):
    assert norm_gain.shape[0] == 1, "single-layer trunk"
    w_nat, w_dil = _prepare_weights(w_in[0])
    weights = (norm_gain[0], mem_norm_gain[0], w_nat, w_dil, w_mem_kv[0].astype(BF16),
               _neighbour_table(rpb[0]), _alibi_coefficients(),
               w_proj_a[0].astype(BF16), w_proj_b[0].astype(BF16), w_proj_m[0].astype(BF16),
               w_out[0].astype(BF16), final_norm_gain)
    return (_trunk(x_prompt, mem_prompt, weights), _trunk(x_sample, mem_sample, weights))
```

```python
import functools

import numpy as np
import jax
import jax.numpy as jnp
from jax import lax
from jax.experimental import pallas as pl
from jax.experimental.pallas import tpu as pltpu

F32 = jnp.float32
BF16 = jnp.bfloat16

D_MODEL = 1024
SEQ = 2048
N_MEM = 256
GRID_W = 64
ROWS = SEQ // GRID_W
HEAD_DIM = 64
A_GROUPS = ((128, 1), (512, 4), (2048, 16))
A_HEADS = 8
NA_ROWS = 8
NA_COLS = 16
M_HEADS = 4
M_HEAD_DIM = 128
RMS_EPS = 1e-6
NEG_INF = -1e30
LOG2_E = 1.4426950408889634

LANES = 128
N_PAIRS = A_HEADS // 2
QT = 128
STRIDE_STEP = 4
A_REACH = 64
B_KEYS = min(NA_ROWS, ROWS) * GRID_W
M_QT = 256
MEM_SEQS_PER_STEP = 4
NORM_ROWS = 256
A_UNROLL = 8
B_UNROLL = 32
M_UNROLL = 8

NAT_COLS = 8192
COL_A_QKV = 0
COL_B_QKV = 1536
COL_AG = 3072
COL_BG = 3584
COL_MQ = 4096
COL_MG = 4608
COL_MERGE = 5120
DIL_COLS = 1536

VMEM_LIMIT = 52 * 1024 * 1024
PROJ_VMEM_LIMIT = 56 * 1024 * 1024


def _gated_ratio(num, den, gate):
    return (num * gate) / (den * (1.0 + jnp.exp(-gate)))


def _dot_nt(a, b):
    return lax.dot_general(a, b, (((1,), (1,)), ((), ())), preferred_element_type=F32)


def _memory_proj_kernel(x_ref, g_ref, w_ref, o_ref):
    g = g_ref[...]
    for c in range(x_ref.shape[1] // NORM_ROWS):
        rows = slice(c * NORM_ROWS, (c + 1) * NORM_ROWS)
        xs = x_ref[0, rows, :]
        ms = jnp.mean(xs * xs, axis=-1, keepdims=True)
        h = ((xs * lax.rsqrt(ms + RMS_EPS)) * g).astype(BF16)
        o_ref[0, rows, :] = jnp.dot(h, w_ref[...], preferred_element_type=F32).astype(BF16)


def _memory_proj(mem, gain, w):
    b = mem.shape[0]
    n = w.shape[1]
    assert b % MEM_SEQS_PER_STEP == 0
    rows = MEM_SEQS_PER_STEP * N_MEM
    out = pl.pallas_call(
        _memory_proj_kernel,
        grid=(b // MEM_SEQS_PER_STEP,),
        in_specs=[
            pl.BlockSpec((1, rows, D_MODEL), lambda i: (i, 0, 0)),
            pl.BlockSpec((1, D_MODEL), lambda i: (0, 0)),
            pl.BlockSpec((D_MODEL, n), lambda i: (0, 0)),
        ],
        out_specs=pl.BlockSpec((1, rows, n), lambda i: (i, 0, 0)),
        out_shape=jax.ShapeDtypeStruct((b // MEM_SEQS_PER_STEP, rows, n), BF16),
        compiler_params=pltpu.CompilerParams(dimension_semantics=("arbitrary",), vmem_limit_bytes=VMEM_LIMIT),
        name="memory_proj",
    )(mem.reshape(b // MEM_SEQS_PER_STEP, rows, D_MODEL), gain.reshape(1, D_MODEL), w)
    return out.reshape(b, N_MEM, n)


NAT_TN = 1024
N_NAT_STEPS = NAT_COLS // NAT_TN
DIL_TN = 768
N_DIL_STEPS = DIL_COLS // DIL_TN
PROJ_MT = 512


def _fused_proj_kernel(x_ref, xs_ref, g_ref, gs_ref, wn_ref, wd_ref, on_ref, od_ref,
                       hn_ref, h4_ref, h16_ref, inv_ref, sa_ref, sb_ref):
    assert N_NAT_STEPS == D_MODEL // LANES, "one lane slab is regrouped per natural-order column step"
    j = pl.program_id(1)
    quarter = SEQ // STRIDE_STEP

    @pl.when(j == 0)
    def _():
        g = g_ref[...]
        for c in range(SEQ // NORM_ROWS):
            sl = slice(c * NORM_ROWS, (c + 1) * NORM_ROWS)
            xs = x_ref[0, sl, :]
            inv = lax.rsqrt(jnp.mean(xs * xs, axis=-1, keepdims=True) + RMS_EPS)
            inv_ref[sl, :] = jnp.broadcast_to(inv, (NORM_ROWS, LANES))
            hn_ref[sl, :] = ((xs * inv) * g).astype(BF16)

    @pl.when(j < N_NAT_STEPS)
    def _():
        assert SEQ // PROJ_MT == STRIDE_STEP
        sa_ref[...] = (xs_ref[0] * inv_ref[...]) * gs_ref[...]
        for c in range(SEQ // PROJ_MT):
            sl = slice(c * PROJ_MT, (c + 1) * PROJ_MT)
            on_ref[0, sl, :] = jnp.dot(hn_ref[sl, :], wn_ref[...], preferred_element_type=F32).astype(BF16)
            part = sa_ref[pl.ds(c, quarter, stride=STRIDE_STEP), :]
            sb_ref[c * quarter:(c + 1) * quarter, :] = part
            h4_ref[j, c * quarter:(c + 1) * quarter, :] = part.astype(BF16)
            for rb in range(STRIDE_STEP):
                slot = c * STRIDE_STEP + rb
                h16_ref[j, slot * QT:(slot + 1) * QT, :] = (
                    sb_ref[pl.ds(c * quarter + rb, QT, stride=STRIDE_STEP), :].astype(BF16))

    for group, h_ref in enumerate((h4_ref, h16_ref)):
        first = N_NAT_STEPS + group * N_DIL_STEPS

        @pl.when((j >= first) & (j < first + N_DIL_STEPS))
        def _(h_ref=h_ref):
            for c in range(SEQ // PROJ_MT):
                sl = slice(c * PROJ_MT, (c + 1) * PROJ_MT)
                h = jnp.concatenate([h_ref[k, sl, :] for k in range(N_NAT_STEPS)], axis=1)
                od_ref[0, 0, sl, :] = jnp.dot(h, wd_ref[0], preferred_element_type=F32).astype(BF16)


def _fused_proj(x, gain, w_nat, w_dil):
    b = x.shape[0]

    def dil_step(j):
        return jnp.clip(j - N_NAT_STEPS, 0, 2 * N_DIL_STEPS - 1)

    return pl.pallas_call(
        _fused_proj_kernel,
        grid=(b, N_NAT_STEPS + 2 * N_DIL_STEPS),
        in_specs=[
            pl.BlockSpec((1, SEQ, D_MODEL), lambda i, j: (i, 0, 0)),
            pl.BlockSpec((1, SEQ, LANES), lambda i, j: (i, 0, jnp.minimum(j, N_NAT_STEPS - 1))),
            pl.BlockSpec((1, D_MODEL), lambda i, j: (0, 0)),
            pl.BlockSpec((1, LANES), lambda i, j: (0, jnp.minimum(j, N_NAT_STEPS - 1))),
            pl.BlockSpec((D_MODEL, NAT_TN), lambda i, j: (0, jnp.minimum(j, N_NAT_STEPS - 1))),
            pl.BlockSpec((1, D_MODEL, DIL_TN), lambda i, j: (dil_step(j), 0, 0)),
        ],
        out_specs=[
            pl.BlockSpec((1, SEQ, NAT_TN), lambda i, j: (i, 0, jnp.minimum(j, N_NAT_STEPS - 1))),
            pl.BlockSpec((1, 1, SEQ, DIL_TN),
                         lambda i, j: (i, dil_step(j) // N_DIL_STEPS, 0, dil_step(j) % N_DIL_STEPS)),
        ],
        out_shape=[jax.ShapeDtypeStruct((b, SEQ, NAT_COLS), BF16),
                   jax.ShapeDtypeStruct((b, 2, SEQ, DIL_COLS), BF16)],
        scratch_shapes=[
            pltpu.VMEM((SEQ, D_MODEL), BF16),
            pltpu.VMEM((D_MODEL // LANES, SEQ, LANES), BF16),
            pltpu.VMEM((D_MODEL // LANES, SEQ, LANES), BF16),
            pltpu.VMEM((SEQ, LANES), F32),
            pltpu.VMEM((SEQ, LANES), F32),
            pltpu.VMEM((SEQ, LANES), F32),
        ],
        compiler_params=pltpu.CompilerParams(
            dimension_semantics=("arbitrary", "arbitrary"), vmem_limit_bytes=PROJ_VMEM_LIMIT),
        name="fused_proj",
    )(x, x, gain.reshape(1, D_MODEL), gain.reshape(1, D_MODEL), w_nat, w_dil)


def _pair_attention(q, k, vext, bias):
    rows = q.shape[0]
    head0 = lax.broadcasted_iota(jnp.int32, (rows, LANES), 1) < HEAD_DIM
    zero = jnp.zeros_like(q)
    q2 = jnp.concatenate([jnp.where(head0, q, zero), jnp.where(head0, zero, q)], axis=0)
    s = _dot_nt(q2, k) + bias
    m = jnp.max(s, axis=-1, keepdims=True)
    p = jnp.exp2(s - m).astype(BF16)
    oe = jnp.dot(p, vext, preferred_element_type=F32)
    o = jnp.where(head0, oe[:rows, :LANES], oe[rows:, :LANES])
    l = jnp.where(head0, oe[:rows, LANES:], oe[rows:, LANES:])
    m = jnp.where(head0, m[:rows], m[rows:])
    return o, l, m


def _dilated_kernel(coef_ref, q0_ref, q1_ref, q2_ref, ag_ref, out_ref,
                    vext_ref, bias_ref, o_ref, l_ref, m_ref, stage_ref):
    hp = pl.program_id(0)
    tk = 2 * QT
    n_tiles = SEQ // QT
    quarter = SEQ // STRIDE_STEP
    qkv_refs = (q0_ref, q1_ref, q2_ref)
    g_staged = [d for _, d in A_GROUPS].index(STRIDE_STEP ** 2)

    @pl.when(pl.program_id(1) == 0)
    def _():
        row = lax.broadcasted_iota(jnp.int32, (QT, tk), 0)
        col = lax.broadcasted_iota(jnp.int32, (QT, tk), 1)
        for g, (_, dil) in enumerate(A_GROUPS):
            vext_ref[g, :, LANES:] = jnp.ones((SEQ, LANES), BF16)
            if SEQ // dil > QT:
                variants = ((0, 0, tk), (-A_REACH, 0, tk), (-2 * A_REACH, 0, tk))
            else:
                variants = ((0, 0, QT), (-QT, QT, tk))
            for v, (off, lo, hi) in enumerate(variants):
                rel = jnp.abs(col + off - row)
                relf = rel.astype(F32)
                masked = (rel > A_REACH) | (col < lo) | (col >= hi)
                for hh in range(2):
                    c = coef_ref[g * A_HEADS + 2 * hp + hh]
                    bias_ref[g, v, hh * QT:(hh + 1) * QT, :] = jnp.where(masked, NEG_INF, -c * relf)

    for g, qkv_ref in enumerate(qkv_refs):
        vext_ref[g, :, :LANES] = qkv_ref[:, 2 * LANES:3 * LANES]

    def group_tile(u, g):
        dil = A_GROUPS[g][1]
        qkv_ref = qkv_refs[g]
        sub = SEQ // dil
        nt = sub // QT
        if nt == 1:
            r, t, vidx = u, 0, u % 2
            ks = (u % 2) * -QT
        else:
            r, t = u // nt, u % nt
            ks = jnp.clip(t * QT - A_REACH, 0, sub - tk)
            vidx = jnp.where(t == 0, 0, jnp.where(t == nt - 1, 2, 1))
        qrow = pl.multiple_of(r * sub + t * QT, QT)
        krow = pl.multiple_of(r * sub + ks, A_REACH)
        q = qkv_ref[pl.ds(qrow, QT), 0:LANES]
        k = qkv_ref[pl.ds(krow, tk), LANES:2 * LANES]
        vext = vext_ref[g, pl.ds(krow, tk), :]
        o, l, m = _pair_attention(q, k, vext, bias_ref[g, vidx])
        if dil == 1:
            dst = pl.ds(pl.multiple_of(t * QT, QT), QT)
        elif dil == STRIDE_STEP:
            dst = pl.ds(r + dil * QT * t, QT, stride=dil)
        else:
            dst = pl.ds((r // STRIDE_STEP) * quarter + r % STRIDE_STEP, QT, stride=STRIDE_STEP)
        if dil == STRIDE_STEP ** 2:
            stage_ref[0, dst, :] = o
            stage_ref[1, dst, :] = l
            stage_ref[2, dst, :] = m
        else:
            o_ref[g, dst, :] = o
            l_ref[g, dst, :] = l
            m_ref[g, dst, :] = m

    def tiles(u, carry):
        for g in range(len(A_GROUPS)):
            group_tile(u, g)
        return carry

    lax.fori_loop(0, n_tiles, tiles, 0, unroll=A_UNROLL)

    def interleave(ra, carry):
        src = pl.ds(pl.multiple_of(ra * quarter, quarter), quarter)
        for k, dst_ref in enumerate((o_ref, l_ref, m_ref)):
            dst_ref[g_staged, pl.ds(ra, quarter, stride=STRIDE_STEP), :] = stage_ref[k, src, :]
        return carry

    lax.fori_loop(0, STRIDE_STEP, interleave, 0)

    def combine(u, carry):
        sl = pl.ds(pl.multiple_of(u * QT, QT), QT)
        ms = [m_ref[g, sl, :] for g in range(3)]
        mm = jnp.maximum(jnp.maximum(ms[0], ms[1]), ms[2])
        ws = [jnp.exp2(m - mm) for m in ms]
        num = ws[0] * o_ref[0, sl, :] + ws[1] * o_ref[1, sl, :] + ws[2] * o_ref[2, sl, :]
        den = ws[0] * l_ref[0, sl, :] + ws[1] * l_ref[1, sl, :] + ws[2] * l_ref[2, sl, :]
        gate = ag_ref[0, sl, :].astype(F32)
        out_ref[0, sl, :] = _gated_ratio(num, den, gate).astype(BF16)
        return carry

    lax.fori_loop(0, SEQ // QT, combine, 0, unroll=4)


def _dilated_mixer(coef, proj_nat, proj_dil):
    b = proj_nat.shape[0]
    grid_spec = pltpu.PrefetchScalarGridSpec(
        num_scalar_prefetch=1,
        grid=(N_PAIRS, b),
        in_specs=[
            pl.BlockSpec((None, SEQ, 3 * LANES), lambda j, i, c: (i, 0, COL_A_QKV // (3 * LANES) + j)),
            pl.BlockSpec((None, None, SEQ, 3 * LANES), lambda j, i, c: (i, 0, 0, j)),
            pl.BlockSpec((None, None, SEQ, 3 * LANES), lambda j, i, c: (i, 1, 0, j)),
            pl.BlockSpec((1, SEQ, LANES), lambda j, i, c: (i, 0, COL_AG // LANES + j)),
        ],
        out_specs=pl.BlockSpec((1, SEQ, LANES), lambda j, i, c: (i, 0, j)),
        scratch_shapes=[
            pltpu.VMEM((3, SEQ, 2 * LANES), BF16),
            pltpu.VMEM((3, 3, 2 * QT, 2 * QT), F32),
            pltpu.VMEM((3, SEQ, LANES), F32),
            pltpu.VMEM((3, SEQ, LANES), F32),
            pltpu.VMEM((3, SEQ, LANES), F32),
            pltpu.VMEM((3, SEQ, LANES), F32),
        ],
    )
    return pl.pallas_call(
        _dilated_kernel,
        grid_spec=grid_spec,
        out_shape=jax.ShapeDtypeStruct((b, SEQ, N_PAIRS * LANES), BF16),
        compiler_params=pltpu.CompilerParams(
            dimension_semantics=("arbitrary", "arbitrary"), vmem_limit_bytes=VMEM_LIMIT),
        name="dilated_mixer",
    )(coef, proj_nat, proj_dil, proj_dil, proj_nat)


def _neighbour_kernel(qkv_ref, bg_ref, tab_ref, out_ref, vext_ref):
    vext_ref[:, LANES:] = jnp.ones((SEQ, LANES), BF16)
    vext_ref[:, :LANES] = qkv_ref[0, :, 2 * LANES:3 * LANES]
    wr = min(NA_ROWS, ROWS)

    def tile(r, carry):
        r0 = jnp.clip(r - wr // 2, 0, ROWS - wr)
        krow = pl.multiple_of(r0 * GRID_W, GRID_W)
        sl = pl.ds(pl.multiple_of(r * GRID_W, GRID_W), GRID_W)
        q = qkv_ref[0, sl, 0:LANES]
        k = qkv_ref[0, pl.ds(krow, B_KEYS), LANES:2 * LANES]
        vext = vext_ref[pl.ds(krow, B_KEYS), :]
        o, l, _ = _pair_attention(q, k, vext, tab_ref[0, r - r0])
        gate = bg_ref[0, sl, :].astype(F32)
        out_ref[0, sl, :] = _gated_ratio(o, l, gate).astype(BF16)
        return carry

    lax.fori_loop(0, ROWS, tile, 0, unroll=B_UNROLL)


def _skew(a, n_rows):
    period = a.shape[-1]
    reps = -(-n_rows * (period + 1) // period)
    flat = jnp.tile(a, (1,) * (a.ndim - 1) + (reps,))[..., :n_rows * (period + 1)]
    return flat.reshape(*a.shape[:-1], n_rows, period + 1)


def _neighbour_table(rpb):
    wr = min(NA_ROWS, ROWS)
    pad = GRID_W - NA_COLS
    rpb_pad = jnp.pad(rpb, ((0, 0), (0, 0), (pad, pad)))
    toeplitz = _skew(rpb_pad, GRID_W)[:, :, ::-1, :GRID_W]
    by_row = _skew(toeplitz.transpose(0, 2, 3, 1), wr)[..., ::-1, :wr]
    tab = by_row.transpose(0, 3, 1, 4, 2).reshape(A_HEADS, wr, GRID_W, B_KEYS)
    c = np.arange(GRID_W)
    kc = np.arange(B_KEYS) % GRID_W
    sc = np.clip(c - NA_COLS // 2, 0, GRID_W - NA_COLS)
    col_ok = (kc[None, :] >= sc[:, None]) & (kc[None, :] < sc[:, None] + NA_COLS)
    tab = jnp.where(col_ok[None, None], tab * LOG2_E, NEG_INF)
    tab = tab.reshape(N_PAIRS, 2, wr, GRID_W, B_KEYS).transpose(0, 2, 1, 3, 4)
    return tab.reshape(N_PAIRS, wr, 2 * GRID_W, B_KEYS)


def _neighbour_mixer(proj_nat, table):
    b = proj_nat.shape[0]
    return pl.pallas_call(
        _neighbour_kernel,
        grid=(N_PAIRS, b),
        in_specs=[
            pl.BlockSpec((1, SEQ, 3 * LANES), lambda j, i: (i, 0, COL_B_QKV // (3 * LANES) + j)),
            pl.BlockSpec((1, SEQ, LANES), lambda j, i: (i, 0, COL_BG // LANES + j)),
            pl.BlockSpec((1, min(NA_ROWS, ROWS), 2 * GRID_W, B_KEYS), lambda j, i: (j, 0, 0, 0)),
        ],
        out_specs=pl.BlockSpec((1, SEQ, LANES), lambda j, i: (i, 0, j)),
        out_shape=jax.ShapeDtypeStruct((b, SEQ, N_PAIRS * LANES), BF16),
        scratch_shapes=[pltpu.VMEM((SEQ, 2 * LANES), BF16)],
        compiler_params=pltpu.CompilerParams(
            dimension_semantics=("arbitrary", "arbitrary"), vmem_limit_bytes=VMEM_LIMIT),
        name="neighbour_mixer",
    )(proj_nat, proj_nat, table)


def _memory_kernel(q_ref, mg_ref, mkv_ref, out_ref, vext_ref):
    width = M_HEADS * M_HEAD_DIM
    for h in range(M_HEADS):
        vext_ref[h, :, LANES:] = jnp.ones((N_MEM, LANES), BF16)
        vext_ref[h, :, :LANES] = mkv_ref[0, :, width + h * LANES:width + (h + 1) * LANES]

    def tile(t, carry):
        sl = pl.ds(pl.multiple_of(t * M_QT, M_QT), M_QT)
        for h in range(M_HEADS):
            cols = slice(h * LANES, (h + 1) * LANES)
            s = _dot_nt(q_ref[0, sl, cols], mkv_ref[0, :, cols])
            m = jnp.max(s, axis=-1, keepdims=True)
            p = jnp.exp2(s - m).astype(BF16)
            oe = jnp.dot(p, vext_ref[h], preferred_element_type=F32)
            gate = mg_ref[0, sl, cols].astype(F32)
            out_ref[0, sl, cols] = _gated_ratio(oe[:, :LANES], oe[:, LANES:], gate).astype(BF16)
        return carry

    lax.fori_loop(0, SEQ // M_QT, tile, 0, unroll=M_UNROLL)


def _memory_mixer(proj_nat, mem_kv):
    b = proj_nat.shape[0]
    width = M_HEADS * M_HEAD_DIM
    assert M_HEAD_DIM == LANES and COL_MQ % width == 0 and COL_MG % width == 0
    return pl.pallas_call(
        _memory_kernel,
        grid=(b,),
        in_specs=[
            pl.BlockSpec((1, SEQ, width), lambda i: (i, 0, COL_MQ // width)),
            pl.BlockSpec((1, SEQ, width), lambda i: (i, 0, COL_MG // width)),
            pl.BlockSpec((1, N_MEM, 2 * width), lambda i: (i, 0, 0)),
        ],
        out_specs=pl.BlockSpec((1, SEQ, width), lambda i: (i, 0, 0)),
        out_shape=jax.ShapeDtypeStruct((b, SEQ, width), BF16),
        scratch_shapes=[pltpu.VMEM((M_HEADS, N_MEM, 2 * LANES), BF16)],
        compiler_params=pltpu.CompilerParams(
            dimension_semantics=("arbitrary",), vmem_limit_bytes=VMEM_LIMIT),
        name="memory_mixer",
    )(proj_nat, proj_nat, mem_kv)


def _merge_out_kernel(ga_ref, gb_ref, gm_ref, s0_ref, s1_ref, s2_ref, x_ref,
                      wa_ref, wb_ref, wm_ref, wo_ref, g_ref, out_ref, *, chunk):
    for c in range(out_ref.shape[1] // chunk):
        rows = slice(c * chunk, (c + 1) * chunk)
        merged = None
        for br_ref, w_ref, s_ref in ((ga_ref, wa_ref, s0_ref), (gb_ref, wb_ref, s1_ref),
                                     (gm_ref, wm_ref, s2_ref)):
            branch = jnp.dot(br_ref[0, rows, :], w_ref[...], preferred_element_type=F32)
            term = jax.nn.sigmoid(s_ref[0, rows, :].astype(F32)) * branch
            merged = term if merged is None else merged + term
        y = x_ref[0, rows, :] + jnp.dot(merged.astype(BF16), wo_ref[...], preferred_element_type=F32)
        ms = jnp.mean(y * y, axis=-1, keepdims=True)
        out_ref[0, rows, :] = (y * lax.rsqrt(ms + RMS_EPS)) * g_ref[...]


def _merge_out(ga, gb, gm, proj_nat, x, wa, wb, wm, wo, gain, *, tm=1024, chunk=512):
    b = x.shape[0]
    width = ga.shape[-1]
    br_spec = pl.BlockSpec((1, tm, width), lambda i, j: (i, j, 0))
    w_spec = pl.BlockSpec((width, D_MODEL), lambda i, j: (0, 0))
    gate_specs = [
        pl.BlockSpec((1, tm, D_MODEL), functools.partial(lambda i, j, n: (i, j, COL_MERGE // D_MODEL + n), n=n))
        for n in range(3)
    ]
    return pl.pallas_call(
        functools.partial(_merge_out_kernel, chunk=chunk),
        grid=(b, SEQ // tm),
        in_specs=[br_spec, br_spec, br_spec, *gate_specs,
                  pl.BlockSpec((1, tm, D_MODEL), lambda i, j: (i, j, 0)),
                  w_spec, w_spec, w_spec,
                  pl.BlockSpec((D_MODEL, D_MODEL), lambda i, j: (0, 0)),
                  pl.BlockSpec((1, D_MODEL), lambda i, j: (0, 0))],
        out_specs=pl.BlockSpec((1, tm, D_MODEL), lambda i, j: (i, j, 0)),
        out_shape=jax.ShapeDtypeStruct((b, SEQ, D_MODEL), F32),
        compiler_params=pltpu.CompilerParams(
            dimension_semantics=("arbitrary", "arbitrary"), vmem_limit_bytes=VMEM_LIMIT),
        name="merge_out",
    )(ga, gb, gm, proj_nat, proj_nat, proj_nat, x, wa, wb, wm, wo, gain.reshape(1, D_MODEL))


def _prepare_weights(w_in):
    a_w = 3 * A_HEADS * HEAD_DIM
    b_w = A_HEADS * HEAD_DIM
    qk_scale = HEAD_DIM ** -0.5 * LOG2_E
    scaled = (qk_scale, 1.0, 1.0)
    aq, ak, av = ((w_in[:, i * a_w:(i + 1) * a_w] * scaled[i]).reshape(D_MODEL, 3, N_PAIRS, LANES) for i in range(3))
    a_qkv = jnp.stack([aq, ak, av], axis=3)
    off = 3 * a_w
    ag = w_in[:, off:off + b_w]
    off += b_w
    bq, bk, bv = ((w_in[:, off + i * b_w:off + (i + 1) * b_w] * scaled[i]).reshape(D_MODEL, N_PAIRS, LANES)
                  for i in range(3))
    b_qkv = jnp.stack([bq, bk, bv], axis=2).reshape(D_MODEL, 3 * b_w)
    off += 3 * b_w
    bg = w_in[:, off:off + b_w]
    m_w = M_HEADS * M_HEAD_DIM
    mq = w_in[:, off + b_w:off + b_w + m_w] * (M_HEAD_DIM ** -0.5 * LOG2_E)
    rest = w_in[:, off + b_w + m_w:]
    w_nat = jnp.concatenate([a_qkv[:, 0].reshape(D_MODEL, a_w), b_qkv, ag, bg, mq, rest], axis=1)
    w_dil = a_qkv[:, 1:].reshape(D_MODEL, 2 * N_DIL_STEPS, DIL_TN).transpose(1, 0, 2)
    return w_nat.astype(BF16), w_dil.astype(BF16)


def _alibi_coefficients():
    n_heads = len(A_GROUPS) * A_HEADS
    slopes = jnp.exp2(-8.0 * jnp.arange(1, n_heads + 1, dtype=F32) / n_heads)
    dils = np.repeat(np.array([d for _, d in A_GROUPS], np.float32), A_HEADS)
    return slopes * dils * LOG2_E


def _trunk(x, mem, weights):
    (g_norm, g_mem, w_nat, w_dil, w_mem, table, coef, wa, wb, wm, wo, g_final) = weights
    proj_nat, proj_dil = _fused_proj(x, g_norm, w_nat, w_dil)
    mem_kv = _memory_proj(mem, g_mem, w_mem)
    ga = _dilated_mixer(coef, proj_nat, proj_dil)
    gb = _neighbour_mixer(proj_nat, table)
    gm = _memory_mixer(proj_nat, mem_kv)
    return _merge_out(ga, gb, gm, proj_nat, x, wa, wb, wm, wo, g_final)


def kernel(x_prompt, x_sample, mem_prompt, mem_sample, norm_gain, mem_norm_gain, w_in, w_mem_kv,
           rpb, w_proj_a, w_proj_b, w_proj_m, w_out, final_norm_gain):
    assert norm_gain.shape[0] == 1, "single-layer trunk"
    w_nat, w_dil = _prepare_weights(w_in[0])
    weights = (norm_gain[0], mem_norm_gain[0], w_nat, w_dil, w_mem_kv[0].astype(BF16),
               _neighbour_table(rpb[0]), _alibi_coefficients(),
               w_proj_a[0].astype(BF16), w_proj_b[0].astype(BF16), w_proj_m[0].astype(BF16),
               w_out[0].astype(BF16), final_norm_gain)
    return (_trunk(x_prompt, mem_prompt, weights), _trunk(x_sample, mem_sample, weights))
```

```python
import functools

import numpy as np
import jax
import jax.numpy as jnp
from jax import lax
from jax.experimental import pallas as pl
from jax.experimental.pallas import tpu as pltpu

F32 = jnp.float32
BF16 = jnp.bfloat16

D_MODEL = 1024
SEQ = 2048
N_MEM = 256
GRID_W = 64
ROWS = SEQ // GRID_W
HEAD_DIM = 64
A_GROUPS = ((128, 1), (512, 4), (2048, 16))
A_HEADS = 8
NA_ROWS = 8
NA_COLS = 16
M_HEADS = 4
M_HEAD_DIM = 128
RMS_EPS = 1e-6
NEG_INF = -1e30
LOG2_E = 1.4426950408889634

LANES = 128
N_PAIRS = A_HEADS // 2
QT = 128
STRIDE_STEP = 4
A_REACH = 64
B_KEYS = min(NA_ROWS, ROWS) * GRID_W
M_QT = 256
MEM_SEQS_PER_STEP = 4
NORM_ROWS = 256
A_UNROLL = 16
B_UNROLL = 32
M_UNROLL = 8

NAT_COLS = 8192
COL_A_QKV = 0
COL_B_QKV = 1536
COL_AG = 3072
COL_BG = 3584
COL_MQ = 4096
COL_MG = 4608
COL_MERGE = 5120
DIL_COLS = 1536

VMEM_LIMIT = 52 * 1024 * 1024
PROJ_VMEM_LIMIT = 56 * 1024 * 1024


def _gated_ratio(num, den, gate):
    return (num * gate) / (den * (1.0 + jnp.exp(-gate)))


def _dot_nt(a, b):
    return lax.dot_general(a, b, (((1,), (1,)), ((), ())), preferred_element_type=F32)


def _memory_proj_kernel(x_ref, g_ref, w_ref, o_ref):
    g = g_ref[...]
    for c in range(x_ref.shape[1] // NORM_ROWS):
        rows = slice(c * NORM_ROWS, (c + 1) * NORM_ROWS)
        xs = x_ref[0, rows, :]
        ms = jnp.mean(xs * xs, axis=-1, keepdims=True)
        h = ((xs * lax.rsqrt(ms + RMS_EPS)) * g).astype(BF16)
        o_ref[0, rows, :] = jnp.dot(h, w_ref[...], preferred_element_type=F32).astype(BF16)


def _memory_proj(mem, gain, w):
    b = mem.shape[0]
    n = w.shape[1]
    assert b % MEM_SEQS_PER_STEP == 0
    rows = MEM_SEQS_PER_STEP * N_MEM
    out = pl.pallas_call(
        _memory_proj_kernel,
        grid=(b // MEM_SEQS_PER_STEP,),
        in_specs=[
            pl.BlockSpec((1, rows, D_MODEL), lambda i: (i, 0, 0)),
            pl.BlockSpec((1, D_MODEL), lambda i: (0, 0)),
            pl.BlockSpec((D_MODEL, n), lambda i: (0, 0)),
        ],
        out_specs=pl.BlockSpec((1, rows, n), lambda i: (i, 0, 0)),
        out_shape=jax.ShapeDtypeStruct((b // MEM_SEQS_PER_STEP, rows, n), BF16),
        compiler_params=pltpu.CompilerParams(dimension_semantics=("arbitrary",), vmem_limit_bytes=VMEM_LIMIT),
        name="memory_proj",
    )(mem.reshape(b // MEM_SEQS_PER_STEP, rows, D_MODEL), gain.reshape(1, D_MODEL), w)
    return out.reshape(b, N_MEM, n)


NAT_TN = 1024
N_NAT_STEPS = NAT_COLS // NAT_TN
DIL_TN = 768
N_DIL_STEPS = DIL_COLS // DIL_TN
PROJ_MT = 512


def _fused_proj_kernel(x_ref, xs_ref, g_ref, gs_ref, wn_ref, wd_ref, on_ref, od_ref,
                       hn_ref, h4_ref, h16_ref, inv_ref, sa_ref, sb_ref):
    assert N_NAT_STEPS == D_MODEL // LANES, "one lane slab is regrouped per natural-order column step"
    j = pl.program_id(1)
    quarter = SEQ // STRIDE_STEP

    @pl.when(j == 0)
    def _():
        g = g_ref[...]
        for c in range(SEQ // NORM_ROWS):
            sl = slice(c * NORM_ROWS, (c + 1) * NORM_ROWS)
            xs = x_ref[0, sl, :]
            inv = lax.rsqrt(jnp.mean(xs * xs, axis=-1, keepdims=True) + RMS_EPS)
            inv_ref[sl, :] = jnp.broadcast_to(inv, (NORM_ROWS, LANES))
            hn_ref[sl, :] = ((xs * inv) * g).astype(BF16)

    @pl.when(j < N_NAT_STEPS)
    def _():
        assert SEQ // PROJ_MT == STRIDE_STEP
        sa_ref[...] = (xs_ref[0] * inv_ref[...]) * gs_ref[...]
        for c in range(SEQ // PROJ_MT):
            sl = slice(c * PROJ_MT, (c + 1) * PROJ_MT)
            on_ref[0, sl, :] = jnp.dot(hn_ref[sl, :], wn_ref[...], preferred_element_type=F32).astype(BF16)
            part = sa_ref[pl.ds(c, quarter, stride=STRIDE_STEP), :]
            sb_ref[c * quarter:(c + 1) * quarter, :] = part
            h4_ref[j, c * quarter:(c + 1) * quarter, :] = part.astype(BF16)
            for rb in range(STRIDE_STEP):
                slot = c * STRIDE_STEP + rb
                h16_ref[j, slot * QT:(slot + 1) * QT, :] = (
                    sb_ref[pl.ds(c * quarter + rb, QT, stride=STRIDE_STEP), :].astype(BF16))

    for group, h_ref in enumerate((h4_ref, h16_ref)):
        first = N_NAT_STEPS + group * N_DIL_STEPS

        @pl.when((j >= first) & (j < first + N_DIL_STEPS))
        def _(h_ref=h_ref):
            for c in range(SEQ // PROJ_MT):
                sl = slice(c * PROJ_MT, (c + 1) * PROJ_MT)
                h = jnp.concatenate([h_ref[k, sl, :] for k in range(N_NAT_STEPS)], axis=1)
                od_ref[0, 0, sl, :] = jnp.dot(h, wd_ref[0], preferred_element_type=F32).astype(BF16)


def _fused_proj(x, gain, w_nat, w_dil):
    b = x.shape[0]

    def dil_step(j):
        return jnp.clip(j - N_NAT_STEPS, 0, 2 * N_DIL_STEPS - 1)

    return pl.pallas_call(
        _fused_proj_kernel,
        grid=(b, N_NAT_STEPS + 2 * N_DIL_STEPS),
        in_specs=[
            pl.BlockSpec((1, SEQ, D_MODEL), lambda i, j: (i, 0, 0)),
            pl.BlockSpec((1, SEQ, LANES), lambda i, j: (i, 0, jnp.minimum(j, N_NAT_STEPS - 1))),
            pl.BlockSpec((1, D_MODEL), lambda i, j: (0, 0)),
            pl.BlockSpec((1, LANES), lambda i, j: (0, jnp.minimum(j, N_NAT_STEPS - 1))),
            pl.BlockSpec((D_MODEL, NAT_TN), lambda i, j: (0, jnp.minimum(j, N_NAT_STEPS - 1))),
            pl.BlockSpec((1, D_MODEL, DIL_TN), lambda i, j: (dil_step(j), 0, 0)),
        ],
        out_specs=[
            pl.BlockSpec((1, SEQ, NAT_TN), lambda i, j: (i, 0, jnp.minimum(j, N_NAT_STEPS - 1))),
            pl.BlockSpec((1, 1, SEQ, DIL_TN),
                         lambda i, j: (i, dil_step(j) // N_DIL_STEPS, 0, dil_step(j) % N_DIL_STEPS)),
        ],
        out_shape=[jax.ShapeDtypeStruct((b, SEQ, NAT_COLS), BF16),
                   jax.ShapeDtypeStruct((b, 2, SEQ, DIL_COLS), BF16)],
        scratch_shapes=[
            pltpu.VMEM((SEQ, D_MODEL), BF16),
            pltpu.VMEM((D_MODEL // LANES, SEQ, LANES), BF16),
            pltpu.VMEM((D_MODEL // LANES, SEQ, LANES), BF16),
            pltpu.VMEM((SEQ, LANES), F32),
            pltpu.VMEM((SEQ, LANES), F32),
            pltpu.VMEM((SEQ, LANES), F32),
        ],
        compiler_params=pltpu.CompilerParams(
            dimension_semantics=("arbitrary", "arbitrary"), vmem_limit_bytes=PROJ_VMEM_LIMIT),
        name="fused_proj",
    )(x, x, gain.reshape(1, D_MODEL), gain.reshape(1, D_MODEL), w_nat, w_dil)


def _pair_attention(q, k, vext, bias):
    rows = q.shape[0]
    head0 = lax.broadcasted_iota(jnp.int32, (rows, LANES), 1) < HEAD_DIM
    zero = jnp.zeros_like(q)
    q2 = jnp.concatenate([jnp.where(head0, q, zero), jnp.where(head0, zero, q)], axis=0)
    s = _dot_nt(q2, k) + bias
    m = jnp.max(s, axis=-1, keepdims=True)
    p = jnp.exp2(s - m).astype(BF16)
    oe = jnp.dot(p, vext, preferred_element_type=F32)
    o = jnp.where(head0, oe[:rows, :LANES], oe[rows:, :LANES])
    l = jnp.where(head0, oe[:rows, LANES:], oe[rows:, LANES:])
    m = jnp.where(head0, m[:rows], m[rows:])
    return o, l, m


def _dilated_kernel(coef_ref, q0_ref, q1_ref, q2_ref, ag_ref, out_ref,
                    vext_ref, bias_ref, o_ref, l_ref, m_ref, stage_ref):
    hp = pl.program_id(0)
    tk = 2 * QT
    n_tiles = SEQ // QT
    quarter = SEQ // STRIDE_STEP
    qkv_refs = (q0_ref, q1_ref, q2_ref)
    g_staged = [d for _, d in A_GROUPS].index(STRIDE_STEP ** 2)

    @pl.when(pl.program_id(1) == 0)
    def _():
        row = lax.broadcasted_iota(jnp.int32, (QT, tk), 0)
        col = lax.broadcasted_iota(jnp.int32, (QT, tk), 1)
        for g, (_, dil) in enumerate(A_GROUPS):
            vext_ref[g, :, LANES:] = jnp.ones((SEQ, LANES), BF16)
            if SEQ // dil > QT:
                variants = ((0, 0, tk), (-A_REACH, 0, tk), (-2 * A_REACH, 0, tk))
            else:
                variants = ((0, 0, QT), (-QT, QT, tk))
            for v, (off, lo, hi) in enumerate(variants):
                rel = jnp.abs(col + off - row)
                relf = rel.astype(F32)
                masked = (rel > A_REACH) | (col < lo) | (col >= hi)
                for hh in range(2):
                    c = coef_ref[g * A_HEADS + 2 * hp + hh]
                    bias_ref[g, v, hh * QT:(hh + 1) * QT, :] = jnp.where(masked, NEG_INF, -c * relf)

    for g, qkv_ref in enumerate(qkv_refs):
        vext_ref[g, :, :LANES] = qkv_ref[:, 2 * LANES:3 * LANES]

    def group_tile(u, g):
        dil = A_GROUPS[g][1]
        qkv_ref = qkv_refs[g]
        sub = SEQ // dil
        nt = sub // QT
        if nt == 1:
            r, t, vidx = u, 0, u % 2
            ks = (u % 2) * -QT
        else:
            r, t = u // nt, u % nt
            ks = jnp.clip(t * QT - A_REACH, 0, sub - tk)
            vidx = jnp.where(t == 0, 0, jnp.where(t == nt - 1, 2, 1))
        qrow = pl.multiple_of(r * sub + t * QT, QT)
        krow = pl.multiple_of(r * sub + ks, A_REACH)
        q = qkv_ref[pl.ds(qrow, QT), 0:LANES]
        k = qkv_ref[pl.ds(krow, tk), LANES:2 * LANES]
        vext = vext_ref[g, pl.ds(krow, tk), :]
        o, l, m = _pair_attention(q, k, vext, bias_ref[g, vidx])
        if dil == 1:
            dst = pl.ds(pl.multiple_of(t * QT, QT), QT)
        elif dil == STRIDE_STEP:
            dst = pl.ds(r + dil * QT * t, QT, stride=dil)
        else:
            dst = pl.ds((r // STRIDE_STEP) * quarter + r % STRIDE_STEP, QT, stride=STRIDE_STEP)
        if dil == STRIDE_STEP ** 2:
            stage_ref[0, dst, :] = o
            stage_ref[1, dst, :] = l
            stage_ref[2, dst, :] = m
        else:
            o_ref[g, dst, :] = o
            l_ref[g, dst, :] = l
            m_ref[g, dst, :] = m

    runtime_zero = jnp.minimum(pl.program_id(1), 0)

    def tiles(u, carry):
        for g in range(len(A_GROUPS)):
            group_tile(u + runtime_zero, g)
        return carry

    lax.fori_loop(0, n_tiles, tiles, 0, unroll=A_UNROLL)

    def interleave(ra, carry):
        src = pl.ds(pl.multiple_of(ra * quarter, quarter), quarter)
        for k, dst_ref in enumerate((o_ref, l_ref, m_ref)):
            dst_ref[g_staged, pl.ds(ra, quarter, stride=STRIDE_STEP), :] = stage_ref[k, src, :]
        return carry

    lax.fori_loop(0, STRIDE_STEP, interleave, 0)

    def combine(u, carry):
        sl = pl.ds(pl.multiple_of(u * QT, QT), QT)
        ms = [m_ref[g, sl, :] for g in range(3)]
        mm = jnp.maximum(jnp.maximum(ms[0], ms[1]), ms[2])
        ws = [jnp.exp2(m - mm) for m in ms]
        num = ws[0] * o_ref[0, sl, :] + ws[1] * o_ref[1, sl, :] + ws[2] * o_ref[2, sl, :]
        den = ws[0] * l_ref[0, sl, :] + ws[1] * l_ref[1, sl, :] + ws[2] * l_ref[2, sl, :]
        gate = ag_ref[0, sl, :].astype(F32)
        out_ref[0, sl, :] = _gated_ratio(num, den, gate).astype(BF16)
        return carry

    lax.fori_loop(0, SEQ // QT, combine, 0, unroll=4)


def _dilated_mixer(coef, proj_nat, proj_dil):
    b = proj_nat.shape[0]
    grid_spec = pltpu.PrefetchScalarGridSpec(
        num_scalar_prefetch=1,
        grid=(N_PAIRS, b),
        in_specs=[
            pl.BlockSpec((None, SEQ, 3 * LANES), lambda j, i, c: (i, 0, COL_A_QKV // (3 * LANES) + j)),
            pl.BlockSpec((None, None, SEQ, 3 * LANES), lambda j, i, c: (i, 0, 0, j)),
            pl.BlockSpec((None, None, SEQ, 3 * LANES), lambda j, i, c: (i, 1, 0, j)),
            pl.BlockSpec((1, SEQ, LANES), lambda j, i, c: (i, 0, COL_AG // LANES + j)),
        ],
        out_specs=pl.BlockSpec((1, SEQ, LANES), lambda j, i, c: (i, 0, j)),
        scratch_shapes=[
            pltpu.VMEM((3, SEQ, 2 * LANES), BF16),
            pltpu.VMEM((3, 3, 2 * QT, 2 * QT), F32),
            pltpu.VMEM((3, SEQ, LANES), F32),
            pltpu.VMEM((3, SEQ, LANES), F32),
            pltpu.VMEM((3, SEQ, LANES), F32),
            pltpu.VMEM((3, SEQ, LANES), F32),
        ],
    )
    return pl.pallas_call(
        _dilated_kernel,
        grid_spec=grid_spec,
        out_shape=jax.ShapeDtypeStruct((b, SEQ, N_PAIRS * LANES), BF16),
        compiler_params=pltpu.CompilerParams(
            dimension_semantics=("arbitrary", "arbitrary"), vmem_limit_bytes=VMEM_LIMIT),
        name="dilated_mixer",
    )(coef, proj_nat, proj_dil, proj_dil, proj_nat)


def _neighbour_kernel(qkv_ref, bg_ref, tab_ref, out_ref, vext_ref):
    vext_ref[:, LANES:] = jnp.ones((SEQ, LANES), BF16)
    vext_ref[:, :LANES] = qkv_ref[0, :, 2 * LANES:3 * LANES]
    wr = min(NA_ROWS, ROWS)

    def tile(r, carry):
        r0 = jnp.clip(r - wr // 2, 0, ROWS - wr)
        krow = pl.multiple_of(r0 * GRID_W, GRID_W)
        sl = pl.ds(pl.multiple_of(r * GRID_W, GRID_W), GRID_W)
        q = qkv_ref[0, sl, 0:LANES]
        k = qkv_ref[0, pl.ds(krow, B_KEYS), LANES:2 * LANES]
        vext = vext_ref[pl.ds(krow, B_KEYS), :]
        o, l, _ = _pair_attention(q, k, vext, tab_ref[0, r - r0])
        gate = bg_ref[0, sl, :].astype(F32)
        out_ref[0, sl, :] = _gated_ratio(o, l, gate).astype(BF16)
        return carry

    lax.fori_loop(0, ROWS, tile, 0, unroll=B_UNROLL)


def _skew(a, n_rows):
    period = a.shape[-1]
    reps = -(-n_rows * (period + 1) // period)
    flat = jnp.tile(a, (1,) * (a.ndim - 1) + (reps,))[..., :n_rows * (period + 1)]
    return flat.reshape(*a.shape[:-1], n_rows, period + 1)


def _neighbour_table(rpb):
    wr = min(NA_ROWS, ROWS)
    pad = GRID_W - NA_COLS
    rpb_pad = jnp.pad(rpb, ((0, 0), (0, 0), (pad, pad)))
    toeplitz = _skew(rpb_pad, GRID_W)[:, :, ::-1, :GRID_W]
    by_row = _skew(toeplitz.transpose(0, 2, 3, 1), wr)[..., ::-1, :wr]
    tab = by_row.transpose(0, 3, 1, 4, 2).reshape(A_HEADS, wr, GRID_W, B_KEYS)
    c = np.arange(GRID_W)
    kc = np.arange(B_KEYS) % GRID_W
    sc = np.clip(c - NA_COLS // 2, 0, GRID_W - NA_COLS)
    col_ok = (kc[None, :] >= sc[:, None]) & (kc[None, :] < sc[:, None] + NA_COLS)
    tab = jnp.where(col_ok[None, None], tab * LOG2_E, NEG_INF)
    tab = tab.reshape(N_PAIRS, 2, wr, GRID_W, B_KEYS).transpose(0, 2, 1, 3, 4)
    return tab.reshape(N_PAIRS, wr, 2 * GRID_W, B_KEYS)


def _neighbour_mixer(proj_nat, table):
    b = proj_nat.shape[0]
    return pl.pallas_call(
        _neighbour_kernel,
        grid=(N_PAIRS, b),
        in_specs=[
            pl.BlockSpec((1, SEQ, 3 * LANES), lambda j, i: (i, 0, COL_B_QKV // (3 * LANES) + j)),
            pl.BlockSpec((1, SEQ, LANES), lambda j, i: (i, 0, COL_BG // LANES + j)),
            pl.BlockSpec((1, min(NA_ROWS, ROWS), 2 * GRID_W, B_KEYS), lambda j, i: (j, 0, 0, 0)),
        ],
        out_specs=pl.BlockSpec((1, SEQ, LANES), lambda j, i: (i, 0, j)),
        out_shape=jax.ShapeDtypeStruct((b, SEQ, N_PAIRS * LANES), BF16),
        scratch_shapes=[pltpu.VMEM((SEQ, 2 * LANES), BF16)],
        compiler_params=pltpu.CompilerParams(
            dimension_semantics=("arbitrary", "arbitrary"), vmem_limit_bytes=VMEM_LIMIT),
        name="neighbour_mixer",
    )(proj_nat, proj_nat, table)


def _memory_kernel(q_ref, mg_ref, mkv_ref, out_ref, vext_ref):
    width = M_HEADS * M_HEAD_DIM
    for h in range(M_HEADS):
        vext_ref[h, :, LANES:] = jnp.ones((N_MEM, LANES), BF16)
        vext_ref[h, :, :LANES] = mkv_ref[0, :, width + h * LANES:width + (h + 1) * LANES]

    def tile(t, carry):
        sl = pl.ds(pl.multiple_of(t * M_QT, M_QT), M_QT)
        for h in range(M_HEADS):
            cols = slice(h * LANES, (h + 1) * LANES)
            s = _dot_nt(q_ref[0, sl, cols], mkv_ref[0, :, cols])
            m = jnp.max(s, axis=-1, keepdims=True)
            p = jnp.exp2(s - m).astype(BF16)
            oe = jnp.dot(p, vext_ref[h], preferred_element_type=F32)
            gate = mg_ref[0, sl, cols].astype(F32)
            out_ref[0, sl, cols] = _gated_ratio(oe[:, :LANES], oe[:, LANES:], gate).astype(BF16)
        return carry

    lax.fori_loop(0, SEQ // M_QT, tile, 0, unroll=M_UNROLL)


def _memory_mixer(proj_nat, mem_kv):
    b = proj_nat.shape[0]
    width = M_HEADS * M_HEAD_DIM
    assert M_HEAD_DIM == LANES and COL_MQ % width == 0 and COL_MG % width == 0
    return pl.pallas_call(
        _memory_kernel,
        grid=(b,),
        in_specs=[
            pl.BlockSpec((1, SEQ, width), lambda i: (i, 0, COL_MQ // width)),
            pl.BlockSpec((1, SEQ, width), lambda i: (i, 0, COL_MG // width)),
            pl.BlockSpec((1, N_MEM, 2 * width), lambda i: (i, 0, 0)),
        ],
        out_specs=pl.BlockSpec((1, SEQ, width), lambda i: (i, 0, 0)),
        out_shape=jax.ShapeDtypeStruct((b, SEQ, width), BF16),
        scratch_shapes=[pltpu.VMEM((M_HEADS, N_MEM, 2 * LANES), BF16)],
        compiler_params=pltpu.CompilerParams(
            dimension_semantics=("arbitrary",), vmem_limit_bytes=VMEM_LIMIT),
        name="memory_mixer",
    )(proj_nat, proj_nat, mem_kv)


def _merge_out_kernel(ga_ref, gb_ref, gm_ref, s0_ref, s1_ref, s2_ref, x_ref,
                      wa_ref, wb_ref, wm_ref, wo_ref, g_ref, out_ref, *, chunk):
    for c in range(out_ref.shape[1] // chunk):
        rows = slice(c * chunk, (c + 1) * chunk)
        merged = None
        for br_ref, w_ref, s_ref in ((ga_ref, wa_ref, s0_ref), (gb_ref, wb_ref, s1_ref),
                                     (gm_ref, wm_ref, s2_ref)):
            branch = jnp.dot(br_ref[0, rows, :], w_ref[...], preferred_element_type=F32)
            term = jax.nn.sigmoid(s_ref[0, rows, :].astype(F32)) * branch
            merged = term if merged is None else merged + term
        y = x_ref[0, rows, :] + jnp.dot(merged.astype(BF16), wo_ref[...], preferred_element_type=F32)
        ms = jnp.mean(y * y, axis=-1, keepdims=True)
        out_ref[0, rows, :] = (y * lax.rsqrt(ms + RMS_EPS)) * g_ref[...]


def _merge_out(ga, gb, gm, proj_nat, x, wa, wb, wm, wo, gain, *, tm=1024, chunk=512):
    b = x.shape[0]
    width = ga.shape[-1]
    br_spec = pl.BlockSpec((1, tm, width), lambda i, j: (i, j, 0))
    w_spec = pl.BlockSpec((width, D_MODEL), lambda i, j: (0, 0))
    gate_specs = [
        pl.BlockSpec((1, tm, D_MODEL), functools.partial(lambda i, j, n: (i, j, COL_MERGE // D_MODEL + n), n=n))
        for n in range(3)
    ]
    return pl.pallas_call(
        functools.partial(_merge_out_kernel, chunk=chunk),
        grid=(b, SEQ // tm),
        in_specs=[br_spec, br_spec, br_spec, *gate_specs,
                  pl.BlockSpec((1, tm, D_MODEL), lambda i, j: (i, j, 0)),
                  w_spec, w_spec, w_spec,
                  pl.BlockSpec((D_MODEL, D_MODEL), lambda i, j: (0, 0)),
                  pl.BlockSpec((1, D_MODEL), lambda i, j: (0, 0))],
        out_specs=pl.BlockSpec((1, tm, D_MODEL), lambda i, j: (i, j, 0)),
        out_shape=jax.ShapeDtypeStruct((b, SEQ, D_MODEL), F32),
        compiler_params=pltpu.CompilerParams(
            dimension_semantics=("arbitrary", "arbitrary"), vmem_limit_bytes=VMEM_LIMIT),
        name="merge_out",
    )(ga, gb, gm, proj_nat, proj_nat, proj_nat, x, wa, wb, wm, wo, gain.reshape(1, D_MODEL))


def _prepare_weights(w_in):
    a_w = 3 * A_HEADS * HEAD_DIM
    b_w = A_HEADS * HEAD_DIM
    qk_scale = HEAD_DIM ** -0.5 * LOG2_E
    scaled = (qk_scale, 1.0, 1.0)
    aq, ak, av = ((w_in[:, i * a_w:(i + 1) * a_w] * scaled[i]).reshape(D_MODEL, 3, N_PAIRS, LANES) for i in range(3))
    a_qkv = jnp.stack([aq, ak, av], axis=3)
    off = 3 * a_w
    ag = w_in[:, off:off + b_w]
    off += b_w
    bq, bk, bv = ((w_in[:, off + i * b_w:off + (i + 1) * b_w] * scaled[i]).reshape(D_MODEL, N_PAIRS, LANES)
                  for i in range(3))
    b_qkv = jnp.stack([bq, bk, bv], axis=2).reshape(D_MODEL, 3 * b_w)
    off += 3 * b_w
    bg = w_in[:, off:off + b_w]
    m_w = M_HEADS * M_HEAD_DIM
    mq = w_in[:, off + b_w:off + b_w + m_w] * (M_HEAD_DIM ** -0.5 * LOG2_E)
    rest = w_in[:, off + b_w + m_w:]
    w_nat = jnp.concatenate([a_qkv[:, 0].reshape(D_MODEL, a_w), b_qkv, ag, bg, mq, rest], axis=1)
    w_dil = a_qkv[:, 1:].reshape(D_MODEL, 2 * N_DIL_STEPS, DIL_TN).transpose(1, 0, 2)
    return w_nat.astype(BF16), w_dil.astype(BF16)


def _alibi_coefficients():
    n_heads = len(A_GROUPS) * A_HEADS
    slopes = jnp.exp2(-8.0 * jnp.arange(1, n_heads + 1, dtype=F32) / n_heads)
    dils = np.repeat(np.array([d for _, d in A_GROUPS], np.float32), A_HEADS)
    return slopes * dils * LOG2_E


def _trunk(x, mem, weights):
    (g_norm, g_mem, w_nat, w_dil, w_mem, table, coef, wa, wb, wm, wo, g_final) = weights
    proj_nat, proj_dil = _fused_proj(x, g_norm, w_nat, w_dil)
    mem_kv = _memory_proj(mem, g_mem, w_mem)
    ga = _dilated_mixer(coef, proj_nat, proj_dil)
    gb = _neighbour_mixer(proj_nat, table)
    gm = _memory_mixer(proj_nat, mem_kv)
    return _merge_out(ga, gb, gm, proj_nat, x, wa, wb, wm, wo, g_final)


def kernel(x_prompt, x_sample, mem_prompt, mem_sample, norm_gain, mem_norm_gain, w_in, w_mem_kv,
           rpb, w_proj_a, w_proj_b, w_proj_m, w_out, final_norm_gain):
    assert norm_gain.shape[0] == 1, "single-layer trunk"
    w_nat, w_dil = _prepare_weights(w_in[0])
    weights = (norm_gain[0], mem_norm_gain[0], w_nat, w_dil, w_mem_kv[0].astype(BF16),
               _neighbour_table(rpb[0]), _alibi_coefficients(),
               w_proj_a[0].astype(BF16), w_proj_b[0].astype(BF16), w_proj_m[0].astype(BF16),
               w_out[0].astype(BF16), final_norm_gain)
    return (_trunk(x_prompt, mem_prompt, weights), _trunk(x_sample, mem_sample, weights))
```
